```python
import jax, jax.numpy as jnp
from jax import lax
import numpy as np

D_MODEL = 4096
BATCH = 1
SEQ = 8192
DEPTH = 1

HEAD_DIM = 128
FOX_HEADS = D_MODEL // (2 * HEAD_DIM)
DSA_HEADS = D_MODEL // (2 * HEAD_DIM)
FOX_WIDTH = FOX_HEADS * HEAD_DIM
DSA_WIDTH = DSA_HEADS * HEAD_DIM
MIX_WIDTH = FOX_WIDTH + DSA_WIDTH
IDX_HEADS = 16
IDX_DIM = 64
DSA_TOPK_MAX = 256
Q_BLOCK = 128
ROPE_THETA = 10000.0
NORM_EPS = 1e-6
PEER_HEADS = 8
PEER_NKEYS = 128
PEER_EXPERTS = PEER_NKEYS * PEER_NKEYS
PEER_QDIM = 128
PEER_SUBDIM = PEER_QDIM // 2
PEER_TOPK = 16
PEER_TOK_BLOCK = 64

IN_SIZES = (FOX_WIDTH, FOX_WIDTH, FOX_WIDTH, FOX_HEADS,
            DSA_WIDTH, HEAD_DIM, HEAD_DIM,
            IDX_HEADS * IDX_DIM, IDX_DIM, IDX_HEADS)
IN_WIDTH = FOX_WIDTH * 3 + FOX_HEADS + DSA_WIDTH + 2 * HEAD_DIM + IDX_HEADS * IDX_DIM + IDX_DIM + IDX_HEADS

kernel_name = "hymba_fox_dsa_peer_block"


def _split_points(sizes):
    pts, acc = [], 0
    for s in sizes[:-1]:
        acc += s
        pts.append(acc)
    return pts


def rmsnorm(x, g):
    xf = x.astype(jnp.float32)
    y = xf * lax.rsqrt(jnp.mean(xf * xf, axis=-1, keepdims=True) + NORM_EPS)
    return (y * g.astype(jnp.float32)).astype(x.dtype)


def rope(x, positions):
    d = x.shape[-1]
    inv = ROPE_THETA ** (-jnp.arange(0, d, 2, dtype=jnp.float32) / d)
    ang = positions.astype(jnp.float32)[..., None] * inv
    cos = jnp.cos(ang)[:, :, None, :]
    sin = jnp.sin(ang)[:, :, None, :]
    xf = x.astype(jnp.float32)
    x1, x2 = xf[..., : d // 2], xf[..., d // 2:]
    return jnp.concatenate([x1 * cos - x2 * sin, x2 * cos + x1 * sin], axis=-1).astype(x.dtype)


def fox_attention(q, k, v, log_f):
    B, S, H, hd = q.shape
    c = jnp.cumsum(log_f, axis=1).transpose(0, 2, 1)
    scale = hd ** -0.5
    kpos = jnp.arange(S)

    def block(i):
        start = i * Q_BLOCK
        qb = lax.dynamic_slice_in_dim(q, start, Q_BLOCK, axis=1)
        cb = lax.dynamic_slice_in_dim(c, start, Q_BLOCK, axis=2)
        s = jnp.einsum('bthd,bshd->bhts', qb, k, preferred_element_type=jnp.float32) * scale
        s = s + cb[..., None] - c[:, :, None, :]
        qpos = start + jnp.arange(Q_BLOCK)
        s = jnp.where(kpos[None, :] <= qpos[:, None], s, -jnp.inf)
        p = jax.nn.softmax(s, axis=-1).astype(v.dtype)
        return jnp.einsum('bhts,bshd->bthd', p, v)

    out = lax.map(block, jnp.arange(S // Q_BLOCK))
    return out.transpose(1, 0, 2, 3, 4).reshape(B, S, H, hd)


def dsa_attention(q, k, v, q_idx, k_idx, w_idx, topk):
    B, S, H, hd = q.shape
    scale = hd ** -0.5
    idx_scale = IDX_DIM ** -0.5
    w_scale = IDX_HEADS ** -0.5
    kpos = jnp.arange(S)
    bidx = jnp.arange(B)[:, None, None]

    def block(i):
        start = i * Q_BLOCK
        qb = lax.dynamic_slice_in_dim(q, start, Q_BLOCK, axis=1)
        qib = lax.dynamic_slice_in_dim(q_idx, start, Q_BLOCK, axis=1)
        wb = lax.dynamic_slice_in_dim(w_idx, start, Q_BLOCK, axis=1).astype(jnp.float32) * w_scale
        qpos = start + jnp.arange(Q_BLOCK)
        logits = jnp.einsum('bthd,bsd->bths', qib, k_idx, preferred_element_type=jnp.float32) * idx_scale
        score = jnp.einsum('bths,bth->bts', jax.nn.relu(logits), wb)
        score = jnp.where(kpos[None, :] <= qpos[:, None], score, -jnp.inf)
        _, sel = lax.top_k(score, topk)
        k_sel = k[bidx, sel]
        v_sel = v[bidx, sel]
        s = jnp.einsum('bthd,btkd->bhtk', qb, k_sel, preferred_element_type=jnp.float32) * scale
        valid = sel <= qpos[None, :, None]
        s = jnp.where(valid[:, None], s, -jnp.inf)
        p = jax.nn.softmax(s, axis=-1).astype(v.dtype)
        return jnp.einsum('bhtk,btkd->bthd', p, v_sel)

    out = lax.map(block, jnp.arange(S // Q_BLOCK))
    return out.transpose(1, 0, 2, 3, 4).reshape(B, S, H, hd)


def peer(x, w_q, keys1, keys2, u, v):
    B, S, D = x.shape
    T = B * S
    xt = x.reshape(T, D)
    q = (xt @ w_q).reshape(T, PEER_HEADS, 2, PEER_SUBDIM)
    s1 = jnp.einsum('thd,hnd->thn', q[:, :, 0], keys1, preferred_element_type=jnp.float32)
    s2 = jnp.einsum('thd,hnd->thn', q[:, :, 1], keys2, preferred_element_type=jnp.float32)
    v1, i1 = lax.top_k(s1, PEER_TOPK)
    v2, i2 = lax.top_k(s2, PEER_TOPK)
    cand = (v1[..., :, None] + v2[..., None, :]).reshape(T, PEER_HEADS, PEER_TOPK * PEER_TOPK)
    cand_idx = (i1[..., :, None] * PEER_NKEYS + i2[..., None, :]).reshape(T, PEER_HEADS, PEER_TOPK * PEER_TOPK)
    best, pos = lax.top_k(cand, PEER_TOPK)
    expert = jnp.take_along_axis(cand_idx, pos, axis=-1)
    gate = jax.nn.softmax(best, axis=-1)
    nb = T // PEER_TOK_BLOCK

    def block(args):
        xb, eb, gb = args
        ub = u[eb]
        h = jnp.einsum('td,thkd->thk', xb, ub, preferred_element_type=jnp.float32)
        a = (gb * jax.nn.gelu(h, approximate=False)).astype(xb.dtype)
        return jnp.einsum('thk,thkd->td', a, v[eb])

    out = lax.map(block, (xt.reshape(nb, PEER_TOK_BLOCK, D),
                          expert.reshape(nb, PEER_TOK_BLOCK, PEER_HEADS, PEER_TOPK),
                          gate.reshape(nb, PEER_TOK_BLOCK, PEER_HEADS, PEER_TOPK)))
    return out.reshape(B, S, D)


def setup_inputs(seed: int = 0) -> dict:
    key = jax.random.key(seed)
    ks = jax.random.split(key, 16)
    f32 = jnp.float32
    n = lambda k, shape, s: jax.random.normal(k, shape, f32) * s
    return {
        "x": jax.random.normal(ks[0], (BATCH, SEQ, D_MODEL), f32),
        "positions": jnp.broadcast_to(jnp.arange(SEQ, dtype=jnp.int32), (BATCH, SEQ)),
        "ln1_g": 1.0 + n(ks[1], (DEPTH, D_MODEL), 0.02),
        "w_in": n(ks[2], (DEPTH, D_MODEL, IN_WIDTH), D_MODEL ** -0.5),
        "fox_forget_b": jax.random.uniform(ks[3], (DEPTH, FOX_HEADS), f32, 1.0, 5.0),
        "fox_qn_g": 1.0 + n(ks[4], (DEPTH, HEAD_DIM), 0.02),
        "fox_kn_g": 1.0 + n(ks[5], (DEPTH, HEAD_DIM), 0.02),
        "dsa_qn_g": 1.0 + n(ks[6], (DEPTH, HEAD_DIM), 0.02),
        "dsa_kn_g": 1.0 + n(ks[7], (DEPTH, HEAD_DIM), 0.02),
        "w_o": n(ks[8], (DEPTH, MIX_WIDTH, D_MODEL), MIX_WIDTH ** -0.5),
        "ln2_g": 1.0 + n(ks[9], (DEPTH, D_MODEL), 0.02),
        "peer_wq": n(ks[10], (DEPTH, D_MODEL, PEER_HEADS * PEER_QDIM), D_MODEL ** -0.5),
        "peer_keys1": n(ks[11], (DEPTH, PEER_HEADS, PEER_NKEYS, PEER_SUBDIM), PEER_SUBDIM ** -0.5),
        "peer_keys2": n(ks[12], (DEPTH, PEER_HEADS, PEER_NKEYS, PEER_SUBDIM), PEER_SUBDIM ** -0.5),
        "peer_u": n(ks[13], (DEPTH, PEER_EXPERTS, D_MODEL), D_MODEL ** -0.5),
        "peer_v": n(ks[14], (DEPTH, PEER_EXPERTS, D_MODEL), (PEER_HEADS * PEER_TOPK) ** -0.5),
    }


def reference(x, positions, ln1_g, w_in, fox_forget_b, fox_qn_g, fox_kn_g, dsa_qn_g, dsa_kn_g,
              w_o, ln2_g, peer_wq, peer_keys1, peer_keys2, peer_u, peer_v):
    B, S, D = x.shape
    topk = min(DSA_TOPK_MAX, S // 4)
    splits = _split_points(IN_SIZES)
    for l in range(DEPTH):
        h = rmsnorm(x, ln1_g[l])
        proj = h @ w_in[l]
        fq, fk, fv, ff, dq, dk, dv, iq, ik, iw = jnp.split(proj, splits, axis=-1)
        fq = rmsnorm(fq.reshape(B, S, FOX_HEADS, HEAD_DIM), fox_qn_g[l])
        fk = rmsnorm(fk.reshape(B, S, FOX_HEADS, HEAD_DIM), fox_kn_g[l])
        fv = fv.reshape(B, S, FOX_HEADS, HEAD_DIM)
        log_f = jax.nn.log_sigmoid(ff.astype(jnp.float32) + fox_forget_b[l].astype(jnp.float32))
        fox_out = fox_attention(fq, fk, fv, log_f)
        dq = rope(rmsnorm(dq.reshape(B, S, DSA_HEADS, HEAD_DIM), dsa_qn_g[l]), positions)
        dk = rope(rmsnorm(dk.reshape(B, S, 1, HEAD_DIM), dsa_kn_g[l]), positions)[:, :, 0]
        iq = rope(iq.reshape(B, S, IDX_HEADS, IDX_DIM), positions)
        ik = rope(ik.reshape(B, S, 1, IDX_DIM), positions)[:, :, 0]
        dsa_out = dsa_attention(dq, dk, dv, iq, ik, iw, topk)
        mix = jnp.concatenate([fox_out.reshape(B, S, FOX_WIDTH), dsa_out.reshape(B, S, DSA_WIDTH)], axis=-1)
        x = x + mix @ w_o[l]
        h2 = rmsnorm(x, ln2_g[l])
        x = x + peer(h2, peer_wq[l], peer_keys1[l], peer_keys2[l], peer_u[l], peer_v[l])
    return x
```

```python
import functools

import numpy as np
import jax
import jax.numpy as jnp
from jax import lax
from jax.experimental import pallas as pl
from jax.experimental.pallas import tpu as pltpu

F32 = jnp.float32
BF16 = jnp.bfloat16
I32 = jnp.int32

HEAD_DIM = 128
IDX_HEADS = 16
IDX_DIM = 64
DSA_TOPK_MAX = 256
ROPE_THETA = 10000.0
NORM_EPS = 1e-6
PEER_HEADS = 8
PEER_NKEYS = 128
PEER_SUBDIM = 64
PEER_TOPK = 16
LOG2E = 1.4426950408889634
LANES = 128

NEG_INF_KEY = -2139095041
INT32_MIN = -2147483648

NT_DIMS = (((1,), (1,)), ((), ()))
NN_DIMS = (((1,), (0,)), ((), ()))


def _params(vmem_mb, *sem):
    return pltpu.CompilerParams(dimension_semantics=sem, vmem_limit_bytes=vmem_mb * 1024 * 1024)


def _rmsnorm_body(x_ref, g_ref, o_ref):
    x = x_ref[...].astype(F32)
    y = x * lax.rsqrt(jnp.mean(x * x, axis=-1, keepdims=True) + NORM_EPS)
    o_ref[...] = (y * g_ref[...]).astype(o_ref.dtype)


def _rmsnorm(x2d, g, out_dtype, tm=256):
    S, D = x2d.shape
    tm = min(tm, S)
    return pl.pallas_call(
        _rmsnorm_body,
        grid=(S // tm,),
        in_specs=[pl.BlockSpec((tm, D), lambda i: (i, 0)), pl.BlockSpec((1, D), lambda i: (0, 0))],
        out_specs=pl.BlockSpec((tm, D), lambda i: (i, 0)),
        out_shape=jax.ShapeDtypeStruct((S, D), out_dtype),
        compiler_params=_params(40, "arbitrary"),
        name="rmsnorm",
    )(x2d, g.reshape(1, D).astype(F32))


def _matmul_body(a_ref, b_ref, *rest, nt, has_res):
    o_ref = rest[-1]
    acc = lax.dot_general(a_ref[...], b_ref[...], NT_DIMS if nt else NN_DIMS, preferred_element_type=F32)
    if has_res:
        acc = acc + rest[0][...]
    o_ref[...] = acc.astype(o_ref.dtype)


def _matmul(a, b, *, nt=False, res=None, out_dtype=F32, tm=1024, tn=512, name="matmul"):
    M, K = a.shape
    N = b.shape[0] if nt else b.shape[1]
    tm, tn = min(tm, M), min(tn, N)
    assert M % tm == 0 and N % tn == 0
    in_specs = [
        pl.BlockSpec((tm, K), lambda i, j: (i, 0)),
        pl.BlockSpec((tn, K), lambda i, j: (j, 0)) if nt else pl.BlockSpec((K, tn), lambda i, j: (0, j)),
    ]
    args = [a, b]
    if res is not None:
        in_specs.append(pl.BlockSpec((tm, tn), lambda i, j: (i, j)))
        args.append(res)
    return pl.pallas_call(
        functools.partial(_matmul_body, nt=nt, has_res=res is not None),
        grid=(M // tm, N // tn),
        in_specs=in_specs,
        out_specs=pl.BlockSpec((tm, tn), lambda i, j: (i, j)),
        out_shape=jax.ShapeDtypeStruct((M, N), out_dtype),
        compiler_params=_params(48, "arbitrary", "arbitrary"),
        name=name,
    )(*args)


def _proj_layout(D):
    fh = D // (2 * HEAD_DIM)
    W = fh * HEAD_DIM
    off = {}
    pos = 0
    for name, width in (("fq", W), ("fk", W), ("fv", W), ("dq", W), ("iq", IDX_HEADS * IDX_DIM),
                        ("dk", HEAD_DIM), ("dv", HEAD_DIM), ("misc", LANES), ("ff", LANES)):
        off[name] = pos
        pos += width
    total = -(-pos // 512) * 512
    return fh, W, off, total


def _post_body(proj_ref, pos_ref, fb_ref, fqg_ref, fkg_ref, dqg_ref, dkg_ref, inv1_ref, sgn1_ref, inv2_ref, sgn2_ref,
               fq_ref, fk_ref, fv_ref, dq_ref, dk_ref, dv_ref, iq_ref, ik_ref, iw_ref, nc_ref, carry_ref,
               *, fh, off, tm):
    @pl.when(pl.program_id(0) == 0)
    def _():
        carry_ref[...] = jnp.zeros_like(carry_ref)

    qscale = (HEAD_DIM ** -0.5) * LOG2E
    pos = pos_ref[...].astype(F32)
    ang1 = pos * inv1_ref[...]
    cos1, sin1 = jnp.cos(ang1), jnp.sin(ang1) * sgn1_ref[...]
    ang2 = pos * inv2_ref[...]
    cos2, sin2 = jnp.cos(ang2), jnp.sin(ang2) * sgn2_ref[...]
    lane = lax.broadcasted_iota(I32, (tm, LANES), 1)
    lo_half = (lane & (IDX_DIM - 1)) < (IDX_DIM // 2)

    def seg(name, j=0):
        s = off[name] + j * LANES
        return proj_ref[:, s:s + LANES]

    def norm(xh, g_ref):
        return xh * lax.rsqrt(jnp.mean(xh * xh, axis=-1, keepdims=True) + NORM_EPS) * g_ref[...]

    def rope_full(xh):
        return xh * cos1 + pltpu.roll(xh, HEAD_DIM // 2, 1) * sin1

    def rope_idx(xh):
        r = jnp.where(lo_half, pltpu.roll(xh, LANES - IDX_DIM // 2, 1), pltpu.roll(xh, IDX_DIM // 2, 1))
        return xh * cos2 + r * sin2

    for h in range(fh):
        hs = slice(h * LANES, (h + 1) * LANES)
        fq_ref[:, hs] = (norm(seg("fq", h), fqg_ref) * qscale).astype(BF16)
        fk_ref[:, hs] = norm(seg("fk", h), fkg_ref).astype(BF16)
        fv_ref[:, hs] = seg("fv", h).astype(BF16)
        dq_ref[:, hs] = (rope_full(norm(seg("dq", h), dqg_ref)) * qscale).astype(BF16)
    dk_ref[...] = rope_full(norm(seg("dk"), dkg_ref)).astype(BF16)
    dv_ref[...] = seg("dv").astype(BF16)

    for j in range(IDX_HEADS // 2):
        blk = rope_idx(seg("iq", j)) * (IDX_DIM ** -0.5)
        iq_ref[2 * j] = blk[:, :IDX_DIM].astype(BF16)
        iq_ref[2 * j + 1] = blk[:, IDX_DIM:].astype(BF16)
    misc = seg("misc")
    ik_ref[...] = rope_idx(misc)[:, :IDX_DIM].astype(BF16)
    iw_ref[...] = misc[:, IDX_DIM:IDX_DIM + IDX_HEADS] * (IDX_HEADS ** -0.5)

    z = seg("ff") + fb_ref[...]
    c = jnp.minimum(z, 0.0) - jnp.log1p(jnp.exp(-jnp.abs(z)))
    row = lax.broadcasted_iota(I32, (tm, LANES), 0)
    s = 1
    while s < tm:
        c = c + jnp.where(row >= s, pltpu.roll(c, s, 0), 0.0)
        s *= 2
    c = c + carry_ref[...]
    carry_ref[...] = c[tm - 1:tm, :]
    nc_ref[...] = ((-LOG2E) * c).T[:fh, :]


def _post(proj, positions, fb, fqg, fkg, dqg, dkg, D, tm=128):
    S, NP = proj.shape
    fh, W, off, total = _proj_layout(D)
    assert total == NP
    tm = min(tm, S)

    def pad_lanes(v):
        return jnp.zeros((1, LANES), F32).at[0, :v.shape[0]].set(v.astype(F32))

    half = HEAD_DIM // 2
    inv_full = ROPE_THETA ** (-jnp.arange(0, HEAD_DIM, 2, dtype=F32) / HEAD_DIM)
    inv1 = jnp.concatenate([inv_full, inv_full]).reshape(1, LANES)
    sgn1 = jnp.concatenate([-jnp.ones(half, F32), jnp.ones(half, F32)]).reshape(1, LANES)
    inv_idx = ROPE_THETA ** (-jnp.arange(0, IDX_DIM, 2, dtype=F32) / IDX_DIM)
    inv2 = jnp.tile(inv_idx, LANES // (IDX_DIM // 2)).reshape(1, LANES)
    q = IDX_DIM // 2
    sgn2 = jnp.tile(jnp.concatenate([-jnp.ones(q, F32), jnp.ones(q, F32)]), LANES // IDX_DIM).reshape(1, LANES)

    row = lambda w: pl.BlockSpec((tm, w), lambda i: (i, 0))
    const = pl.BlockSpec((1, LANES), lambda i: (0, 0))
    out_shape = (
        jax.ShapeDtypeStruct((S, W), BF16), jax.ShapeDtypeStruct((S, W), BF16), jax.ShapeDtypeStruct((S, W), BF16),
        jax.ShapeDtypeStruct((S, W), BF16), jax.ShapeDtypeStruct((S, HEAD_DIM), BF16),
        jax.ShapeDtypeStruct((S, HEAD_DIM), BF16), jax.ShapeDtypeStruct((IDX_HEADS, S, IDX_DIM), BF16),
        jax.ShapeDtypeStruct((S, IDX_DIM), BF16), jax.ShapeDtypeStruct((S, IDX_HEADS), F32),
        jax.ShapeDtypeStruct((fh, S), F32),
    )
    out_specs = (
        row(W), row(W), row(W), row(W), row(HEAD_DIM), row(HEAD_DIM),
        pl.BlockSpec((IDX_HEADS, tm, IDX_DIM), lambda i: (0, i, 0)), row(IDX_DIM), row(IDX_HEADS),
        pl.BlockSpec((fh, tm), lambda i: (0, i)),
    )
    return pl.pallas_call(
        functools.partial(_post_body, fh=fh, off=off, tm=tm),
        grid=(S // tm,),
        in_specs=[row(NP), row(1)] + [const] * 9,
        out_specs=out_specs,
        out_shape=out_shape,
        scratch_shapes=[pltpu.VMEM((1, LANES), F32)],
        compiler_params=_params(40, "arbitrary"),
        name="post",
    )(proj, positions.reshape(S, 1).astype(I32), pad_lanes(fb), fqg.reshape(1, LANES).astype(F32),
      fkg.reshape(1, LANES).astype(F32), dqg.reshape(1, LANES).astype(F32), dkg.reshape(1, LANES).astype(F32),
      inv1, sgn1, inv2, sgn2)


def _softmax_step(s, v, m_ref, l_ref, acc_ref):
    m_prev = m_ref[...]
    m_new = jnp.maximum(m_prev, jnp.max(s, axis=-1, keepdims=True))
    m_safe = jnp.where(m_new == -jnp.inf, 0.0, m_new)
    alpha = jnp.exp2(m_prev - m_safe)
    p = jnp.exp2(s - m_safe)
    l_ref[...] = alpha * l_ref[...] + jnp.sum(p, axis=-1, keepdims=True)
    acc_ref[...] = alpha * acc_ref[...] + jnp.dot(p.astype(BF16), v, preferred_element_type=F32)
    m_ref[...] = m_new


def _fox_body(q_ref, k_ref, v_ref, nc_ref, o_ref, m_ref, l_ref, acc_ref, *, tb):
    qi = pl.program_id(1)
    m_ref[...] = jnp.full_like(m_ref, -jnp.inf)
    l_ref[...] = jnp.zeros_like(l_ref)
    acc_ref[...] = jnp.zeros_like(acc_ref)
    q = q_ref[...]

    def step(ki, masked):
        ks = pl.multiple_of(ki * tb, tb)
        k = k_ref[pl.ds(ks, tb), :]
        v = v_ref[pl.ds(ks, tb), :]
        s = lax.dot_general(q, k, NT_DIMS, preferred_element_type=F32) + nc_ref[ki]
        if masked:
            r = lax.broadcasted_iota(I32, (tb, tb), 0)
            c = lax.broadcasted_iota(I32, (tb, tb), 1)
            s = jnp.where(c <= r, s, -jnp.inf)
        _softmax_step(s, v, m_ref, l_ref, acc_ref)

    def loop_body(ki, carry):
        step(ki, False)
        return carry

    lax.fori_loop(0, qi, loop_body, 0)
    step(qi, True)
    o_ref[...] = (acc_ref[...] / l_ref[...]).astype(o_ref.dtype)


def _fox_attention(fq, fk, fv, negc, tb=512):
    S, W = fq.shape
    fh = W // HEAD_DIM
    tb = min(tb, S)
    nb = S // tb
    nc = negc.reshape(fh, nb, 1, tb)
    return pl.pallas_call(
        functools.partial(_fox_body, tb=tb),
        grid=(fh, nb),
        in_specs=[
            pl.BlockSpec((tb, HEAD_DIM), lambda h, i: (i, h)),
            pl.BlockSpec((S, HEAD_DIM), lambda h, i: (0, h)),
            pl.BlockSpec((S, HEAD_DIM), lambda h, i: (0, h)),
            pl.BlockSpec((None, nb, 1, tb), lambda h, i: (h, 0, 0, 0)),
        ],
        out_specs=pl.BlockSpec((tb, HEAD_DIM), lambda h, i: (i, h)),
        out_shape=jax.ShapeDtypeStruct((S, W), BF16),
        scratch_shapes=[pltpu.VMEM((tb, 1), F32), pltpu.VMEM((tb, 1), F32), pltpu.VMEM((tb, HEAD_DIM), F32)],
        compiler_params=_params(40, "arbitrary", "arbitrary"),
        name="fox_attention",
    )(fq, fk, fv, nc)


def _sortable_key(x):
    b = lax.bitcast_convert_type(x, I32)
    return b ^ ((b >> 31) & 0x7FFFFFFF)


def _dsa_body(iq_ref, ik_ref, iw_ref, dq_ref, dk_ref, dv_ref, o_ref,
              keys_ref, w_ref, q_ref, m_ref, l_ref, acc_ref, *, tq, tk, topk, nh):
    qi = pl.program_id(0)
    nkb = (qi * tq) // tk + 1
    row_g = qi * tq + lax.broadcasted_iota(I32, (tq, tk), 0)
    col_l = lax.broadcasted_iota(I32, (tq, tk), 1)

    iq_all = iq_ref[...].reshape(IDX_HEADS * tq, IDX_DIM)
    for h in range(IDX_HEADS):
        w_ref[h * tq:(h + 1) * tq, :] = iw_ref[:, h:h + 1]

    def score_chunk(kb, carry):
        ks = pl.multiple_of(kb * tk, tk)
        logits = lax.dot_general(iq_all, ik_ref[pl.ds(ks, tk), :], NT_DIMS, preferred_element_type=F32)
        sc = jnp.sum((jnp.maximum(logits, 0.0) * w_ref[...]).reshape(IDX_HEADS, tq, tk), axis=0)
        sc = jnp.where(ks + col_l <= row_g, sc, -jnp.inf)
        keys_ref[kb] = _sortable_key(sc)
        return carry

    lax.fori_loop(0, nkb, score_chunk, 0)

    def bit_step(i, thr):
        cand = thr + lax.shift_left(jnp.int32(1), 31 - i)

        def count_chunk(kb, cnt):
            hit = jnp.where(keys_ref[kb] >= cand, 1.0, 0.0)
            return cnt + jnp.sum(hit, axis=-1, keepdims=True)

        cnt = lax.fori_loop(0, nkb, count_chunk, jnp.zeros((tq, 1), F32))
        return jnp.where(cnt >= float(topk), cand, thr)

    thr = lax.fori_loop(0, 32, bit_step, jnp.full((tq, 1), INT32_MIN, I32))
    thr = jnp.maximum(thr, NEG_INF_KEY + 1)

    for h in range(nh):
        q_ref[h * tq:(h + 1) * tq, :] = dq_ref[:, h * HEAD_DIM:(h + 1) * HEAD_DIM]
    m_ref[...] = jnp.full_like(m_ref, -jnp.inf)
    l_ref[...] = jnp.zeros_like(l_ref)
    acc_ref[...] = jnp.zeros_like(acc_ref)

    def attn_chunk(kb, carry):
        ks = pl.multiple_of(kb * tk, tk)
        k = dk_ref[pl.ds(ks, tk), :]
        v = dv_ref[pl.ds(ks, tk), :]
        s = lax.dot_general(q_ref[...], k, NT_DIMS, preferred_element_type=F32)
        drop = jnp.where(keys_ref[kb] >= thr, 0.0, -jnp.inf)
        s = (s.reshape(nh, tq, tk) + drop[None]).reshape(nh * tq, tk)
        _softmax_step(s, v, m_ref, l_ref, acc_ref)
        return carry

    lax.fori_loop(0, nkb, attn_chunk, 0)
    out = acc_ref[...] / l_ref[...]
    for h in range(nh):
        o_ref[:, h * HEAD_DIM:(h + 1) * HEAD_DIM] = out[h * tq:(h + 1) * tq, :].astype(o_ref.dtype)


def _dsa_attention(iq, ik, iw, dq, dk, dv, topk, tq=128, tk=512):
    S, W = dq.shape
    nh = W // HEAD_DIM
    tq, tk = min(tq, S), min(tk, S)
    nkb = S // tk
    full = lambda shape: pl.BlockSpec(shape, lambda i: (0,) * len(shape))
    return pl.pallas_call(
        functools.partial(_dsa_body, tq=tq, tk=tk, topk=topk, nh=nh),
        grid=(S // tq,),
        in_specs=[
            pl.BlockSpec((IDX_HEADS, tq, IDX_DIM), lambda i: (0, i, 0)),
            full((S, IDX_DIM)),
            pl.BlockSpec((tq, IDX_HEADS), lambda i: (i, 0)),
            pl.BlockSpec((tq, W), lambda i: (i, 0)),
            full((S, HEAD_DIM)),
            full((S, HEAD_DIM)),
        ],
        out_specs=pl.BlockSpec((tq, W), lambda i: (i, 0)),
        out_shape=jax.ShapeDtypeStruct((S, W), BF16),
        scratch_shapes=[
            pltpu.VMEM((nkb, tq, tk), I32),
            pltpu.VMEM((IDX_HEADS * tq, 1), F32),
            pltpu.VMEM((nh * tq, HEAD_DIM), BF16),
            pltpu.VMEM((nh * tq, 1), F32),
            pltpu.VMEM((nh * tq, 1), F32),
            pltpu.VMEM((nh * tq, HEAD_DIM), F32),
        ],
        compiler_params=_params(56, "arbitrary"),
        name="dsa_attention",
    )(iq, ik, iw, dq, dk, dv)


def _top_rows(s, n):
    rows = []
    for _ in range(n):
        m = jnp.max(s, axis=0, keepdims=True)
        rows.append(m)
        s = jnp.where(s == m, -jnp.inf, s)
    return rows


def _route_body(qt_ref, kk_ref, s1_ref, e1_ref, s2_ref, e2_ref, tau_ref):
    for h in range(PEER_HEADS):
        qh = qt_ref[h * LANES:(h + 1) * LANES, :]
        sc = jnp.dot(kk_ref[h], qh, preferred_element_type=F32, precision=lax.Precision.HIGHEST)
        s1, s2 = sc[:PEER_NKEYS], sc[PEER_NKEYS:]
        v1 = _top_rows(s1, PEER_TOPK)
        v2 = jnp.concatenate(_top_rows(s2, PEER_TOPK), axis=0)
        cand = jnp.concatenate([v1[a] + v2 for a in range(PEER_TOPK)], axis=0)
        best = _top_rows(cand, PEER_TOPK)
        z = sum(jnp.exp(b - best[0]) for b in best)
        s1_ref[h] = s1
        e1_ref[h] = jnp.exp(s1 - v1[0]) / z
        s2_ref[h] = s2
        e2_ref[h] = jnp.exp(s2 - v2[0:1])
        tau_ref[h:h + 1, :] = best[PEER_TOPK - 1]


def _peer_route(qt, kk, ts=256):
    NQ, S = qt.shape
    ts = min(ts, S)
    shp = jax.ShapeDtypeStruct((PEER_HEADS, PEER_NKEYS, S), F32)
    spec = pl.BlockSpec((PEER_HEADS, PEER_NKEYS, ts), lambda i: (0, 0, i))
    return pl.pallas_call(
        _route_body,
        grid=(S // ts,),
        in_specs=[pl.BlockSpec((NQ, ts), lambda i: (0, i)),
                  pl.BlockSpec((PEER_HEADS, 2 * PEER_NKEYS, LANES), lambda i: (0, 0, 0))],
        out_specs=(spec, spec, spec, spec, pl.BlockSpec((PEER_HEADS, ts), lambda i: (0, i))),
        out_shape=(shp, shp, shp, shp, jax.ShapeDtypeStruct((PEER_HEADS, S), F32)),
        compiler_params=_params(40, "arbitrary"),
        name="peer_route",
    )(qt, kk)


ROUTE_GROUP = 8


def _experts_body(x_ref, u_ref, vt_ref, s1_ref, e1_ref, s2_ref, e2_ref, tau_ref, o_ref, *, nsub):
    eb = pl.program_id(1)

    @pl.when(eb == 0)
    def _():
        o_ref[...] = jnp.zeros_like(o_ref)

    ht = lax.dot_general(u_ref[...], x_ref[...], NT_DIMS, preferred_element_type=F32)
    first = (eb * nsub) % ROUTE_GROUP
    parts = []
    for j in range(nsub):
        gate = jnp.zeros((PEER_NKEYS, ht.shape[1]), F32)
        for h in range(PEER_HEADS):
            s1 = s1_ref[h, pl.ds(first + j, 1), :]
            e1 = e1_ref[h, pl.ds(first + j, 1), :]
            routed = (s1 + s2_ref[h]) >= tau_ref[h:h + 1, :]
            gate = gate + jnp.where(routed, e2_ref[h] * e1, 0.0)
        hj = ht[j * PEER_NKEYS:(j + 1) * PEER_NKEYS, :]
        act = 0.5 * hj * (1.0 + lax.erf(hj * (2.0 ** -0.5)))
        parts.append((gate * act).astype(BF16))
    at = jnp.concatenate(parts, axis=0)
    o_ref[...] += jnp.dot(vt_ref[...], at, preferred_element_type=F32)


def _peer_experts(h2, u, vt, s1, e1, s2, e2, tau, tb=512, te=512):
    S, D = h2.shape
    E = u.shape[0]
    tb = min(tb, S)
    nsub = te // PEER_NKEYS
    assert ROUTE_GROUP % nsub == 0

    def regroup(t):
        return t.reshape(PEER_HEADS, PEER_NKEYS // ROUTE_GROUP, ROUTE_GROUP, S).transpose(1, 0, 2, 3)

    grouped = pl.BlockSpec((None, PEER_HEADS, ROUTE_GROUP, tb), lambda i, j: ((j * nsub) // ROUTE_GROUP, 0, 0, i))
    whole = pl.BlockSpec((PEER_HEADS, PEER_NKEYS, tb), lambda i, j: (0, 0, i))
    return pl.pallas_call(
        functools.partial(_experts_body, nsub=nsub),
        grid=(S // tb, E // te),
        in_specs=[
            pl.BlockSpec((tb, D), lambda i, j: (i, 0)),
            pl.BlockSpec((te, D), lambda i, j: (j, 0)),
            pl.BlockSpec((D, te), lambda i, j: (0, j)),
            grouped, grouped, whole, whole,
            pl.BlockSpec((PEER_HEADS, tb), lambda i, j: (0, i)),
        ],
        out_specs=pl.BlockSpec((D, tb), lambda i, j: (0, i)),
        out_shape=jax.ShapeDtypeStruct((D, S), F32),
        compiler_params=_params(60, "arbitrary", "arbitrary"),
        name="peer_experts",
    )(h2, u, vt, regroup(s1), regroup(e1), s2, e2, tau)


def _final_body(x_ref, pt_ref, o_ref):
    o_ref[...] = x_ref[...] + pt_ref[...].T


def _final(x1, peer_t, ts=256):
    S, D = x1.shape
    ts = min(ts, S)
    return pl.pallas_call(
        _final_body,
        grid=(S // ts,),
        in_specs=[pl.BlockSpec((ts, D), lambda i: (i, 0)), pl.BlockSpec((D, ts), lambda i: (0, i))],
        out_specs=pl.BlockSpec((ts, D), lambda i: (i, 0)),
        out_shape=jax.ShapeDtypeStruct((S, D), F32),
        compiler_params=_params(40, "arbitrary"),
        name="final_residual",
    )(x1, peer_t)


def _relayout_w_in(w_in, D):
    fh, W, off, total = _proj_layout(D)
    sizes = (W, W, W, fh, W, HEAD_DIM, HEAD_DIM, IDX_HEADS * IDX_DIM, IDX_DIM, IDX_HEADS)
    pts = np.cumsum(sizes)[:-1].tolist()
    fq, fk, fv, ff, dq, dk, dv, iq, ik, iw = jnp.split(w_in, pts, axis=1)
    zeros = lambda n: jnp.zeros((D, n), w_in.dtype)
    misc = jnp.concatenate([ik, iw, zeros(LANES - IDX_DIM - IDX_HEADS)], axis=1)
    ffp = jnp.concatenate([ff, zeros(LANES - fh)], axis=1)
    cols = [fq, fk, fv, dq, iq, dk, dv, misc, ffp]
    used = sum(c.shape[1] for c in cols)
    if total > used:
        cols.append(zeros(total - used))
    return jnp.concatenate(cols, axis=1).astype(BF16)


def _layer(x, positions, ln1_g, w_in, fox_forget_b, fox_qn_g, fox_kn_g, dsa_qn_g, dsa_kn_g,
           w_o, ln2_g, peer_wq, peer_keys1, peer_keys2, peer_u, peer_v):
    S, D = x.shape
    topk = min(DSA_TOPK_MAX, S // 4)

    h1 = _rmsnorm(x, ln1_g, BF16)
    proj = _matmul(h1, _relayout_w_in(w_in, D), name="in_proj")
    fq, fk, fv, dq, dk, dv, iq, ik, iw, negc = _post(proj, positions, fox_forget_b, fox_qn_g, fox_kn_g,
                                                     dsa_qn_g, dsa_kn_g, D)
    fox_out = _fox_attention(fq, fk, fv, negc)
    dsa_out = _dsa_attention(iq, ik, iw, dq, dk, dv, topk)
    mix = jnp.concatenate([fox_out, dsa_out], axis=1)
    x1 = _matmul(mix, w_o.astype(BF16), res=x, name="out_proj")

    h2 = _rmsnorm(x1, ln2_g, BF16)
    qt = _matmul(peer_wq.T.astype(BF16), h2, nt=True, name="peer_query")
    zk = jnp.zeros_like(peer_keys1)
    kk = jnp.concatenate([jnp.concatenate([peer_keys1, zk], axis=2),
                          jnp.concatenate([zk, peer_keys2], axis=2)], axis=1).astype(F32)
    s1, e1, s2, e2, tau = _peer_route(qt, kk)
    peer_t = _peer_experts(h2, peer_u.astype(BF16), peer_v.T.astype(BF16), s1, e1, s2, e2, tau)
    return _final(x1, peer_t)


def kernel(x, positions, ln1_g, w_in, fox_forget_b, fox_qn_g, fox_kn_g, dsa_qn_g, dsa_kn_g, w_o, ln2_g, peer_wq,
           peer_keys1, peer_keys2, peer_u, peer_v):
    B = x.shape[0]
    depth = w_in.shape[0]
    outs = []
    for b in range(B):
        xb = x[b]
        for l in range(depth):
            xb = _layer(xb, positions[b], ln1_g[l], w_in[l], fox_forget_b[l], fox_qn_g[l], fox_kn_g[l],
                        dsa_qn_g[l], dsa_kn_g[l], w_o[l], ln2_g[l], peer_wq[l], peer_keys1[l], peer_keys2[l],
                        peer_u[l], peer_v[l])
        outs.append(xb)
    return jnp.stack(outs, axis=0)
```

```python
import functools

import numpy as np
import jax
import jax.numpy as jnp
from jax import lax
from jax.experimental import pallas as pl
from jax.experimental.pallas import tpu as pltpu

F32 = jnp.float32
BF16 = jnp.bfloat16
I32 = jnp.int32

HEAD_DIM = 128
IDX_HEADS = 16
IDX_DIM = 64
DSA_TOPK_MAX = 256
ROPE_THETA = 10000.0
NORM_EPS = 1e-6
PEER_HEADS = 8
PEER_NKEYS = 128
PEER_SUBDIM = 64
PEER_TOPK = 16
LOG2E = 1.4426950408889634
LANES = 128

NEG_INF_KEY = -2139095041
INT32_MIN = -2147483648

NT_DIMS = (((1,), (1,)), ((), ()))
NN_DIMS = (((1,), (0,)), ((), ()))


def _params(vmem_mb, *sem):
    return pltpu.CompilerParams(dimension_semantics=sem, vmem_limit_bytes=vmem_mb * 1024 * 1024)


def _rmsnorm_body(x_ref, g_ref, o_ref):
    x = x_ref[...].astype(F32)
    y = x * lax.rsqrt(jnp.mean(x * x, axis=-1, keepdims=True) + NORM_EPS)
    o_ref[...] = (y * g_ref[...]).astype(o_ref.dtype)


def _rmsnorm(x2d, g, out_dtype, tm=256):
    S, D = x2d.shape
    tm = min(tm, S)
    return pl.pallas_call(
        _rmsnorm_body,
        grid=(S // tm,),
        in_specs=[pl.BlockSpec((tm, D), lambda i: (i, 0)), pl.BlockSpec((1, D), lambda i: (0, 0))],
        out_specs=pl.BlockSpec((tm, D), lambda i: (i, 0)),
        out_shape=jax.ShapeDtypeStruct((S, D), out_dtype),
        compiler_params=_params(40, "arbitrary"),
        name="rmsnorm",
    )(x2d, g.reshape(1, D).astype(F32))


def _matmul_body(a_ref, b_ref, *rest, nt, has_res):
    o_ref = rest[-1]
    acc = lax.dot_general(a_ref[...], b_ref[...], NT_DIMS if nt else NN_DIMS, preferred_element_type=F32)
    if has_res:
        acc = acc + rest[0][...]
    o_ref[...] = acc.astype(o_ref.dtype)


def _matmul(a, b, *, nt=False, res=None, out_dtype=F32, tm=1024, tn=512, name="matmul"):
    M, K = a.shape
    N = b.shape[0] if nt else b.shape[1]
    tm, tn = min(tm, M), min(tn, N)
    assert M % tm == 0 and N % tn == 0
    in_specs = [
        pl.BlockSpec((tm, K), lambda i, j: (i, 0)),
        pl.BlockSpec((tn, K), lambda i, j: (j, 0)) if nt else pl.BlockSpec((K, tn), lambda i, j: (0, j)),
    ]
    args = [a, b]
    if res is not None:
        in_specs.append(pl.BlockSpec((tm, tn), lambda i, j: (i, j)))
        args.append(res)
    return pl.pallas_call(
        functools.partial(_matmul_body, nt=nt, has_res=res is not None),
        grid=(M // tm, N // tn),
        in_specs=in_specs,
        out_specs=pl.BlockSpec((tm, tn), lambda i, j: (i, j)),
        out_shape=jax.ShapeDtypeStruct((M, N), out_dtype),
        compiler_params=_params(48, "arbitrary", "arbitrary"),
        name=name,
    )(*args)


def _proj_layout(D):
    fh = D // (2 * HEAD_DIM)
    W = fh * HEAD_DIM
    off = {}
    pos = 0
    for name, width in (("fq", W), ("fk", W), ("fv", W), ("dq", W), ("iq", IDX_HEADS * IDX_DIM),
                        ("dk", HEAD_DIM), ("dv", HEAD_DIM), ("misc", LANES), ("ff", LANES)):
        off[name] = pos
        pos += width
    total = -(-pos // 512) * 512
    return fh, W, off, total


def _post_body(proj_ref, pos_ref, fb_ref, fqg_ref, fkg_ref, dqg_ref, dkg_ref, inv1_ref, sgn1_ref, inv2_ref, sgn2_ref,
               fq_ref, fk_ref, fv_ref, dq_ref, dk_ref, dv_ref, iq_ref, ik_ref, iw_ref, nc_ref, carry_ref,
               *, fh, off, tm):
    @pl.when(pl.program_id(0) == 0)
    def _():
        carry_ref[...] = jnp.zeros_like(carry_ref)

    qscale = (HEAD_DIM ** -0.5) * LOG2E
    pos = pos_ref[...].astype(F32)
    ang1 = pos * inv1_ref[...]
    cos1, sin1 = jnp.cos(ang1), jnp.sin(ang1) * sgn1_ref[...]
    ang2 = pos * inv2_ref[...]
    cos2, sin2 = jnp.cos(ang2), jnp.sin(ang2) * sgn2_ref[...]
    lane = lax.broadcasted_iota(I32, (tm, LANES), 1)
    lo_half = (lane & (IDX_DIM - 1)) < (IDX_DIM // 2)

    def seg(name, j=0):
        s = off[name] + j * LANES
        return proj_ref[:, s:s + LANES]

    def norm(xh, g_ref):
        return xh * lax.rsqrt(jnp.mean(xh * xh, axis=-1, keepdims=True) + NORM_EPS) * g_ref[...]

    def rope_full(xh):
        return xh * cos1 + pltpu.roll(xh, HEAD_DIM // 2, 1) * sin1

    def rope_idx(xh):
        r = jnp.where(lo_half, pltpu.roll(xh, LANES - IDX_DIM // 2, 1), pltpu.roll(xh, IDX_DIM // 2, 1))
        return xh * cos2 + r * sin2

    for h in range(fh):
        hs = slice(h * LANES, (h + 1) * LANES)
        fq_ref[:, hs] = (norm(seg("fq", h), fqg_ref) * qscale).astype(BF16)
        fk_ref[:, hs] = norm(seg("fk", h), fkg_ref).astype(BF16)
        fv_ref[:, hs] = seg("fv", h).astype(BF16)
        dq_ref[:, hs] = (rope_full(norm(seg("dq", h), dqg_ref)) * qscale).astype(BF16)
    dk_ref[...] = rope_full(norm(seg("dk"), dkg_ref)).astype(BF16)
    dv_ref[...] = seg("dv").astype(BF16)

    for j in range(IDX_HEADS // 2):
        blk = rope_idx(seg("iq", j)) * (IDX_DIM ** -0.5)
        iq_ref[2 * j] = blk[:, :IDX_DIM].astype(BF16)
        iq_ref[2 * j + 1] = blk[:, IDX_DIM:].astype(BF16)
    misc = seg("misc")
    ik_ref[...] = rope_idx(misc)[:, :IDX_DIM].astype(BF16)
    iw_ref[...] = misc[:, IDX_DIM:IDX_DIM + IDX_HEADS] * (IDX_HEADS ** -0.5)

    z = seg("ff") + fb_ref[...]
    c = jnp.minimum(z, 0.0) - jnp.log1p(jnp.exp(-jnp.abs(z)))
    row = lax.broadcasted_iota(I32, (tm, LANES), 0)
    s = 1
    while s < tm:
        c = c + jnp.where(row >= s, pltpu.roll(c, s, 0), 0.0)
        s *= 2
    c = c + carry_ref[...]
    carry_ref[...] = c[tm - 1:tm, :]
    nc_ref[...] = ((-LOG2E) * c).T[:fh, :]


def _post(proj, positions, fb, fqg, fkg, dqg, dkg, D, tm=128):
    S, NP = proj.shape
    fh, W, off, total = _proj_layout(D)
    assert total == NP
    tm = min(tm, S)

    def pad_lanes(v):
        return jnp.zeros((1, LANES), F32).at[0, :v.shape[0]].set(v.astype(F32))

    half = HEAD_DIM // 2
    inv_full = ROPE_THETA ** (-jnp.arange(0, HEAD_DIM, 2, dtype=F32) / HEAD_DIM)
    inv1 = jnp.concatenate([inv_full, inv_full]).reshape(1, LANES)
    sgn1 = jnp.concatenate([-jnp.ones(half, F32), jnp.ones(half, F32)]).reshape(1, LANES)
    inv_idx = ROPE_THETA ** (-jnp.arange(0, IDX_DIM, 2, dtype=F32) / IDX_DIM)
    inv2 = jnp.tile(inv_idx, LANES // (IDX_DIM // 2)).reshape(1, LANES)
    q = IDX_DIM // 2
    sgn2 = jnp.tile(jnp.concatenate([-jnp.ones(q, F32), jnp.ones(q, F32)]), LANES // IDX_DIM).reshape(1, LANES)

    row = lambda w: pl.BlockSpec((tm, w), lambda i: (i, 0))
    const = pl.BlockSpec((1, LANES), lambda i: (0, 0))
    out_shape = (
        jax.ShapeDtypeStruct((S, W), BF16), jax.ShapeDtypeStruct((S, W), BF16), jax.ShapeDtypeStruct((S, W), BF16),
        jax.ShapeDtypeStruct((S, W), BF16), jax.ShapeDtypeStruct((S, HEAD_DIM), BF16),
        jax.ShapeDtypeStruct((S, HEAD_DIM), BF16), jax.ShapeDtypeStruct((IDX_HEADS, S, IDX_DIM), BF16),
        jax.ShapeDtypeStruct((S, IDX_DIM), BF16), jax.ShapeDtypeStruct((S, IDX_HEADS), F32),
        jax.ShapeDtypeStruct((fh, S), F32),
    )
    out_specs = (
        row(W), row(W), row(W), row(W), row(HEAD_DIM), row(HEAD_DIM),
        pl.BlockSpec((IDX_HEADS, tm, IDX_DIM), lambda i: (0, i, 0)), row(IDX_DIM), row(IDX_HEADS),
        pl.BlockSpec((fh, tm), lambda i: (0, i)),
    )
    return pl.pallas_call(
        functools.partial(_post_body, fh=fh, off=off, tm=tm),
        grid=(S // tm,),
        in_specs=[row(NP), row(1)] + [const] * 9,
        out_specs=out_specs,
        out_shape=out_shape,
        scratch_shapes=[pltpu.VMEM((1, LANES), F32)],
        compiler_params=_params(40, "arbitrary"),
        name="post",
    )(proj, positions.reshape(S, 1).astype(I32), pad_lanes(fb), fqg.reshape(1, LANES).astype(F32),
      fkg.reshape(1, LANES).astype(F32), dqg.reshape(1, LANES).astype(F32), dkg.reshape(1, LANES).astype(F32),
      inv1, sgn1, inv2, sgn2)


def _softmax_step(s, v, m_ref, l_ref, acc_ref):
    m_prev = m_ref[...]
    m_new = jnp.maximum(m_prev, jnp.max(s, axis=-1, keepdims=True))
    m_safe = jnp.where(m_new == -jnp.inf, 0.0, m_new)
    alpha = jnp.exp2(m_prev - m_safe)
    p = jnp.exp2(s - jnp.tile(m_safe, (1, s.shape[1] // LANES)))
    l_ref[...] = alpha * l_ref[...] + jnp.sum(p, axis=-1, keepdims=True)
    acc_ref[...] = alpha * acc_ref[...] + jnp.dot(p.astype(BF16), v, preferred_element_type=F32)
    m_ref[...] = m_new


def _fox_body(q_ref, k_ref, v_ref, nc_ref, o_ref, m_ref, l_ref, acc_ref, *, tb, hp):
    qi = pl.program_id(1)
    m_ref[...] = jnp.full_like(m_ref, -jnp.inf)
    l_ref[...] = jnp.zeros_like(l_ref)
    acc_ref[...] = jnp.zeros_like(acc_ref)

    def step(ki, masked):
        ks = pl.multiple_of(ki * tb, tb)
        for h in range(hp):
            hs = slice(h * HEAD_DIM, (h + 1) * HEAD_DIM)
            k = k_ref[pl.ds(ks, tb), hs]
            v = v_ref[pl.ds(ks, tb), hs]
            s = lax.dot_general(q_ref[:, hs], k, NT_DIMS, preferred_element_type=F32) + nc_ref[h, ki]
            if masked:
                r = lax.broadcasted_iota(I32, (tb, tb), 0)
                c = lax.broadcasted_iota(I32, (tb, tb), 1)
                s = jnp.where(c <= r, s, -jnp.inf)
            _softmax_step(s, v, m_ref.at[h], l_ref.at[h], acc_ref.at[h])

    def loop_body(ki, carry):
        step(ki, False)
        return carry

    lax.fori_loop(0, qi, loop_body, 0)
    step(qi, True)
    for h in range(hp):
        o_ref[:, h * HEAD_DIM:(h + 1) * HEAD_DIM] = (acc_ref[h] / l_ref[h]).astype(o_ref.dtype)


def _fox_attention(fq, fk, fv, negc, tb=512, hp=2):
    S, W = fq.shape
    fh = W // HEAD_DIM
    tb = min(tb, S)
    nb = S // tb
    hp = min(hp, fh)
    nc = negc.reshape(fh, nb, 1, tb)
    wide = hp * HEAD_DIM
    stat = pltpu.VMEM((hp, tb, LANES), F32)
    return pl.pallas_call(
        functools.partial(_fox_body, tb=tb, hp=hp),
        grid=(fh // hp, nb),
        in_specs=[
            pl.BlockSpec((tb, wide), lambda h, i: (i, h)),
            pl.BlockSpec((S, wide), lambda h, i: (0, h)),
            pl.BlockSpec((S, wide), lambda h, i: (0, h)),
            pl.BlockSpec((hp, nb, 1, tb), lambda h, i: (h, 0, 0, 0)),
        ],
        out_specs=pl.BlockSpec((tb, wide), lambda h, i: (i, h)),
        out_shape=jax.ShapeDtypeStruct((S, W), BF16),
        scratch_shapes=[stat, stat, pltpu.VMEM((hp, tb, HEAD_DIM), F32)],
        compiler_params=_params(40, "arbitrary", "arbitrary"),
        name="fox_attention",
    )(fq, fk, fv, nc)


def _sortable_key(x):
    b = lax.bitcast_convert_type(x, I32)
    return b ^ ((b >> 31) & 0x7FFFFFFF)


def _dsa_body(iq_ref, ik_ref, iw_ref, dq_ref, dk_ref, dv_ref, o_ref,
              keys_ref, w_ref, q_ref, m_ref, l_ref, acc_ref, *, tq, tk, topk, nh):
    qi = pl.program_id(0)
    groups = 2 if nh % 2 == 0 else 1
    ig, ag = IDX_HEADS // groups, nh // groups
    nkb = (qi * tq) // tk + 1
    row_g = qi * tq + lax.broadcasted_iota(I32, (tq, tk), 0)
    col_l = lax.broadcasted_iota(I32, (tq, tk), 1)

    for h in range(IDX_HEADS):
        w_ref[h * tq:(h + 1) * tq, :] = iw_ref[:, h:h + 1]

    def score_chunk(kb, carry):
        ks = pl.multiple_of(kb * tk, tk)
        ikb = ik_ref[pl.ds(ks, tk), :]
        sc = None
        for g in range(groups):
            rows = slice(g * ig * tq, (g + 1) * ig * tq)
            iq_g = iq_ref[g * ig:(g + 1) * ig].reshape(ig * tq, IDX_DIM)
            logits = lax.dot_general(iq_g, ikb, NT_DIMS, preferred_element_type=F32)
            part = jnp.sum((jnp.maximum(logits, 0.0) * w_ref[rows, :]).reshape(ig, tq, tk), axis=0)
            sc = part if sc is None else sc + part
        sc = jnp.where(ks + col_l <= row_g, sc, -jnp.inf)
        keys_ref[kb] = _sortable_key(sc)
        return carry

    lax.fori_loop(0, nkb, score_chunk, 0)

    def bit_step(i, thr):
        cand = thr + lax.shift_left(jnp.int32(1), 31 - i)

        def count_chunk(kb, cnt):
            hit = jnp.where(keys_ref[kb] >= cand, 1.0, 0.0)
            return cnt + sum(hit[:, j * LANES:(j + 1) * LANES] for j in range(tk // LANES))

        cnt = lax.fori_loop(0, nkb, count_chunk, jnp.zeros((tq, LANES), F32))
        cnt = jnp.sum(cnt, axis=-1, keepdims=True)
        return jnp.where(cnt >= float(topk), cand, thr)

    thr = lax.fori_loop(0, 32, bit_step, jnp.full((tq, 1), INT32_MIN, I32))
    thr = jnp.maximum(thr, NEG_INF_KEY + 1)

    for h in range(nh):
        q_ref[h * tq:(h + 1) * tq, :] = dq_ref[:, h * HEAD_DIM:(h + 1) * HEAD_DIM]
    m_ref[...] = jnp.full_like(m_ref, -jnp.inf)
    l_ref[...] = jnp.zeros_like(l_ref)
    acc_ref[...] = jnp.zeros_like(acc_ref)

    def attn_chunk(kb, carry):
        ks = pl.multiple_of(kb * tk, tk)
        k = dk_ref[pl.ds(ks, tk), :]
        v = dv_ref[pl.ds(ks, tk), :]
        drop = jnp.where(keys_ref[kb] >= thr, 0.0, -jnp.inf)
        for g in range(groups):
            rows = pl.ds(g * ag * tq, ag * tq)
            s = lax.dot_general(q_ref[rows, :], k, NT_DIMS, preferred_element_type=F32)
            s = (s.reshape(ag, tq, tk) + drop[None]).reshape(ag * tq, tk)
            _softmax_step(s, v, m_ref.at[rows], l_ref.at[rows], acc_ref.at[rows])
        return carry

    lax.fori_loop(0, nkb, attn_chunk, 0)
    out = acc_ref[...] / l_ref[...]
    for h in range(nh):
        o_ref[:, h * HEAD_DIM:(h + 1) * HEAD_DIM] = out[h * tq:(h + 1) * tq, :].astype(o_ref.dtype)


def _dsa_attention(iq, ik, iw, dq, dk, dv, topk, tq=128, tk=512):
    S, W = dq.shape
    nh = W // HEAD_DIM
    tq, tk = min(tq, S), min(tk, S)
    nkb = S // tk
    full = lambda shape: pl.BlockSpec(shape, lambda i: (0,) * len(shape))
    return pl.pallas_call(
        functools.partial(_dsa_body, tq=tq, tk=tk, topk=topk, nh=nh),
        grid=(S // tq,),
        in_specs=[
            pl.BlockSpec((IDX_HEADS, tq, IDX_DIM), lambda i: (0, i, 0)),
            full((S, IDX_DIM)),
            pl.BlockSpec((tq, IDX_HEADS), lambda i: (i, 0)),
            pl.BlockSpec((tq, W), lambda i: (i, 0)),
            full((S, HEAD_DIM)),
            full((S, HEAD_DIM)),
        ],
        out_specs=pl.BlockSpec((tq, W), lambda i: (i, 0)),
        out_shape=jax.ShapeDtypeStruct((S, W), BF16),
        scratch_shapes=[
            pltpu.VMEM((nkb, tq, tk), I32),
            pltpu.VMEM((IDX_HEADS * tq, 1), F32),
            pltpu.VMEM((nh * tq, HEAD_DIM), BF16),
            pltpu.VMEM((nh * tq, LANES), F32),
            pltpu.VMEM((nh * tq, LANES), F32),
            pltpu.VMEM((nh * tq, HEAD_DIM), F32),
        ],
        compiler_params=_params(56, "arbitrary"),
        name="dsa_attention",
    )(iq, ik, iw, dq, dk, dv)


def _top_rows(s, n):
    rows = []
    for _ in range(n):
        m = jnp.max(s, axis=0, keepdims=True)
        rows.append(m)
        s = jnp.where(s == m, -jnp.inf, s)
    return rows


def _route_body(qt_ref, kk_ref, s1_ref, e1_ref, s2_ref, e2_ref, tau_ref):
    for h in range(PEER_HEADS):
        qh = qt_ref[h * LANES:(h + 1) * LANES, :]
        sc = jnp.dot(kk_ref[h], qh, preferred_element_type=F32, precision=lax.Precision.HIGHEST)
        s1, s2 = sc[:PEER_NKEYS], sc[PEER_NKEYS:]
        v1 = _top_rows(s1, PEER_TOPK)
        v2 = jnp.concatenate(_top_rows(s2, PEER_TOPK), axis=0)
        cand = jnp.concatenate([v1[a] + v2 for a in range(PEER_TOPK)], axis=0)
        best = _top_rows(cand, PEER_TOPK)
        z = sum(jnp.exp(b - best[0]) for b in best)
        s1_ref[h] = s1
        e1_ref[h] = jnp.exp(s1 - v1[0]) / z
        s2_ref[h] = s2
        e2_ref[h] = jnp.exp(s2 - v2[0:1])
        tau_ref[h:h + 1, :] = best[PEER_TOPK - 1]


def _peer_route(qt, kk, ts=256):
    NQ, S = qt.shape
    ts = min(ts, S)
    shp = jax.ShapeDtypeStruct((PEER_HEADS, PEER_NKEYS, S), F32)
    spec = pl.BlockSpec((PEER_HEADS, PEER_NKEYS, ts), lambda i: (0, 0, i))
    return pl.pallas_call(
        _route_body,
        grid=(S // ts,),
        in_specs=[pl.BlockSpec((NQ, ts), lambda i: (0, i)),
                  pl.BlockSpec((PEER_HEADS, 2 * PEER_NKEYS, LANES), lambda i: (0, 0, 0))],
        out_specs=(spec, spec, spec, spec, pl.BlockSpec((PEER_HEADS, ts), lambda i: (0, i))),
        out_shape=(shp, shp, shp, shp, jax.ShapeDtypeStruct((PEER_HEADS, S), F32)),
        compiler_params=_params(40, "arbitrary"),
        name="peer_route",
    )(qt, kk)


ROUTE_GROUP = 8


def _experts_body(x_ref, u_ref, v_ref, s1_ref, e1_ref, s2_ref, e2_ref, tau_ref, res_ref, o_ref, *, nsub, nsplit):
    eb = pl.program_id(1)

    @pl.when(eb == 0)
    def _():
        o_ref[...] = jnp.zeros_like(o_ref)

    rr = res_ref.shape[0]
    r0 = pl.multiple_of(eb * rr, rr)
    o_ref[pl.ds(r0, rr), :] += res_ref[...]

    first = (eb * nsub) % ROUTE_GROUP
    tb = x_ref.shape[0]
    per = nsub // nsplit
    for c in range(nsplit):
        es = slice(c * per * PEER_NKEYS, (c + 1) * per * PEER_NKEYS)
        ht = lax.dot_general(u_ref[es, :], x_ref[...], NT_DIMS, preferred_element_type=F32)
        parts = []
        for j in range(per):
            i1 = first + c * per + j
            gate = jnp.zeros((PEER_NKEYS, tb), F32)
            for h in range(PEER_HEADS):
                s1 = s1_ref[h, pl.ds(i1, 1), :]
                e1 = e1_ref[h, pl.ds(i1, 1), :]
                routed = (s1 + s2_ref[h]) >= tau_ref[h:h + 1, :]
                gate = gate + jnp.where(routed, e2_ref[h] * e1, 0.0)
            hj = ht[j * PEER_NKEYS:(j + 1) * PEER_NKEYS, :]
            act = 0.5 * hj * (1.0 + lax.erf(hj * (2.0 ** -0.5)))
            parts.append(gate * act)
        a = jnp.concatenate(parts, axis=0).T.astype(BF16)
        o_ref[...] += jnp.dot(a, v_ref[es, :], preferred_element_type=F32)


def _peer_experts(h2, u, v, s1, e1, s2, e2, tau, res, tb=512, te=512, nsplit=1):
    S, D = h2.shape
    E = u.shape[0]
    tb = min(tb, S)
    nsub = te // PEER_NKEYS
    ne = E // te
    rr = tb // ne
    assert ROUTE_GROUP % nsub == 0 and tb % ne == 0 and rr % 8 == 0

    def regroup(t):
        return t.reshape(PEER_HEADS, PEER_NKEYS // ROUTE_GROUP, ROUTE_GROUP, S).transpose(1, 0, 2, 3)

    grouped = pl.BlockSpec((None, PEER_HEADS, ROUTE_GROUP, tb), lambda i, j: ((j * nsub) // ROUTE_GROUP, 0, 0, i))
    whole = pl.BlockSpec((PEER_HEADS, PEER_NKEYS, tb), lambda i, j: (0, 0, i))
    return pl.pallas_call(
        functools.partial(_experts_body, nsub=nsub, nsplit=nsplit),
        grid=(S // tb, ne),
        in_specs=[
            pl.BlockSpec((tb, D), lambda i, j: (i, 0)),
            pl.BlockSpec((te, D), lambda i, j: (j, 0)),
            pl.BlockSpec((te, D), lambda i, j: (j, 0)),
            grouped, grouped, whole, whole,
            pl.BlockSpec((PEER_HEADS, tb), lambda i, j: (0, i)),
            pl.BlockSpec((rr, D), lambda i, j: (i * ne + j, 0)),
        ],
        out_specs=pl.BlockSpec((tb, D), lambda i, j: (i, 0)),
        out_shape=jax.ShapeDtypeStruct((S, D), F32),
        compiler_params=_params(60, "arbitrary", "arbitrary"),
        name="peer_experts",
    )(h2, u, v, regroup(s1), regroup(e1), s2, e2, tau, res)


def _relayout_w_in(w_in, D):
    fh, W, off, total = _proj_layout(D)
    sizes = (W, W, W, fh, W, HEAD_DIM, HEAD_DIM, IDX_HEADS * IDX_DIM, IDX_DIM, IDX_HEADS)
    pts = np.cumsum(sizes)[:-1].tolist()
    fq, fk, fv, ff, dq, dk, dv, iq, ik, iw = jnp.split(w_in, pts, axis=1)
    zeros = lambda n: jnp.zeros((D, n), w_in.dtype)
    misc = jnp.concatenate([ik, iw, zeros(LANES - IDX_DIM - IDX_HEADS)], axis=1)
    ffp = jnp.concatenate([ff, zeros(LANES - fh)], axis=1)
    cols = [fq, fk, fv, dq, iq, dk, dv, misc, ffp]
    used = sum(c.shape[1] for c in cols)
    if total > used:
        cols.append(zeros(total - used))
    return jnp.concatenate(cols, axis=1).astype(BF16)


def _layer(x, positions, ln1_g, w_in, fox_forget_b, fox_qn_g, fox_kn_g, dsa_qn_g, dsa_kn_g,
           w_o, ln2_g, peer_wq, peer_keys1, peer_keys2, peer_u, peer_v):
    S, D = x.shape
    topk = min(DSA_TOPK_MAX, S // 4)

    h1 = _rmsnorm(x, ln1_g, BF16)
    proj = _matmul(h1, _relayout_w_in(w_in, D), name="in_proj")
    fq, fk, fv, dq, dk, dv, iq, ik, iw, negc = _post(proj, positions, fox_forget_b, fox_qn_g, fox_kn_g,
                                                     dsa_qn_g, dsa_kn_g, D)
    fox_out = _fox_attention(fq, fk, fv, negc)
    dsa_out = _dsa_attention(iq, ik, iw, dq, dk, dv, topk)
    mix = jnp.concatenate([fox_out, dsa_out], axis=1)
    x1 = _matmul(mix, w_o.astype(BF16), res=x, name="out_proj")

    h2 = _rmsnorm(x1, ln2_g, BF16)
    qt = _matmul(peer_wq.T.astype(BF16), h2, nt=True, name="peer_query")
    zk = jnp.zeros_like(peer_keys1)
    kk = jnp.concatenate([jnp.concatenate([peer_keys1, zk], axis=2),
                          jnp.concatenate([zk, peer_keys2], axis=2)], axis=1).astype(F32)
    s1, e1, s2, e2, tau = _peer_route(qt, kk)
    return _peer_experts(h2, peer_u.astype(BF16), peer_v.astype(BF16), s1, e1, s2, e2, tau, x1)


def kernel(x, positions, ln1_g, w_in, fox_forget_b, fox_qn_g, fox_kn_g, dsa_qn_g, dsa_kn_g, w_o, ln2_g, peer_wq,
           peer_keys1, peer_keys2, peer_u, peer_v):
    B = x.shape[0]
    depth = w_in.shape[0]
    outs = []
    for b in range(B):
        xb = x[b]
        for l in range(depth):
            xb = _layer(xb, positions[b], ln1_g[l], w_in[l], fox_forget_b[l], fox_qn_g[l], fox_kn_g[l],
                        dsa_qn_g[l], dsa_kn_g[l], w_o[l], ln2_g[l], peer_wq[l], peer_keys1[l], peer_keys2[l],
                        peer_u[l], peer_v[l])
        outs.append(xb)
    return jnp.stack(outs, axis=0)
```

```python
import functools

import numpy as np
import jax
import jax.numpy as jnp
from jax import lax
from jax.experimental import pallas as pl
from jax.experimental.pallas import tpu as pltpu

F32 = jnp.float32
BF16 = jnp.bfloat16
I32 = jnp.int32

HEAD_DIM = 128
IDX_HEADS = 16
IDX_DIM = 64
DSA_TOPK_MAX = 256
ROPE_THETA = 10000.0
NORM_EPS = 1e-6
PEER_HEADS = 8
PEER_NKEYS = 128
PEER_SUBDIM = 64
PEER_TOPK = 16
LOG2E = 1.4426950408889634
LANES = 128
WIDE = 2 * HEAD_DIM
QSCALE = (HEAD_DIM ** -0.5) * LOG2E

LOGIT_BOUND_PER_GAIN = (HEAD_DIM ** 0.5) * LOG2E
SAFE_LOGIT_SPAN = 60.0

NEG_INF_KEY = -2139095041
INT32_MIN = -2147483648

NT_DIMS = (((1,), (1,)), ((), ()))
NN_DIMS = (((1,), (0,)), ((), ()))


def _params(vmem_mb, *sem):
    return pltpu.CompilerParams(dimension_semantics=sem, vmem_limit_bytes=vmem_mb * 1024 * 1024)


def _as_bf16(x):
    return x if x.dtype == BF16 else x.astype(BF16)


def _rmsnorm_body(x_ref, g_ref, o_ref):
    x = x_ref[...].astype(F32)
    y = x * lax.rsqrt(jnp.mean(x * x, axis=-1, keepdims=True) + NORM_EPS)
    o_ref[...] = (y * g_ref[...]).astype(o_ref.dtype)


def _rmsnorm(x2d, g, out_dtype, tm=256):
    S, D = x2d.shape
    tm = min(tm, S)
    return pl.pallas_call(
        _rmsnorm_body,
        grid=(S // tm,),
        in_specs=[pl.BlockSpec((tm, D), lambda i: (i, 0)), pl.BlockSpec((1, D), lambda i: (0, 0))],
        out_specs=pl.BlockSpec((tm, D), lambda i: (i, 0)),
        out_shape=jax.ShapeDtypeStruct((S, D), out_dtype),
        compiler_params=_params(40, "arbitrary"),
        name="rmsnorm",
    )(x2d, g.reshape(1, D).astype(F32))


def _matmul_body(a_ref, b_ref, o_ref, *, nt):
    acc = lax.dot_general(a_ref[...], _as_bf16(b_ref[...]), NT_DIMS if nt else NN_DIMS, preferred_element_type=F32)
    o_ref[...] = acc.astype(o_ref.dtype)


def _matmul(a, b, *, nt=False, n_cols=None, out_dtype=F32, tm=1024, tn=512, name="matmul"):
    M, K = a.shape
    N = n_cols if n_cols is not None else (b.shape[0] if nt else b.shape[1])
    tm = min(tm, M)
    tn = next(t for t in range(min(tn, N), 0, -LANES) if N % t == 0)
    assert M % tm == 0 and tn % LANES == 0
    return pl.pallas_call(
        functools.partial(_matmul_body, nt=nt),
        grid=(M // tm, N // tn),
        in_specs=[
            pl.BlockSpec((tm, K), lambda i, j: (i, 0)),
            pl.BlockSpec((tn, K), lambda i, j: (j, 0)) if nt else pl.BlockSpec((K, tn), lambda i, j: (0, j)),
        ],
        out_specs=pl.BlockSpec((tm, tn), lambda i, j: (i, j)),
        out_shape=jax.ShapeDtypeStruct((M, N), out_dtype),
        compiler_params=_params(48, "arbitrary", "arbitrary"),
        name=name,
    )(a, b)


def _out_proj_body(a1_ref, a2_ref, b1_ref, b2_ref, r_ref, o_ref):
    acc = jnp.dot(a1_ref[...], _as_bf16(b1_ref[...]), preferred_element_type=F32)
    acc = acc + jnp.dot(a2_ref[...], _as_bf16(b2_ref[...]), preferred_element_type=F32)
    o_ref[...] = acc + r_ref[...]


def _out_proj(a1, a2, w, res, tm=1024, tn=512):
    M, K1 = a1.shape
    N = w.shape[1]
    assert a2.shape == a1.shape and w.shape[0] == 2 * K1
    tm, tn = min(tm, M), min(tn, N)
    lhs = pl.BlockSpec((tm, K1), lambda i, j: (i, 0))
    return pl.pallas_call(
        _out_proj_body,
        grid=(M // tm, N // tn),
        in_specs=[lhs, lhs, pl.BlockSpec((K1, tn), lambda i, j: (0, j)), pl.BlockSpec((K1, tn), lambda i, j: (1, j)),
                  pl.BlockSpec((tm, tn), lambda i, j: (i, j))],
        out_specs=pl.BlockSpec((tm, tn), lambda i, j: (i, j)),
        out_shape=jax.ShapeDtypeStruct((M, N), F32),
        compiler_params=_params(48, "arbitrary", "arbitrary"),
        name="out_proj",
    )(a1, a2, w, w, res)


def _tail_layout(D):
    fh = D // (2 * HEAD_DIM)
    W = fh * HEAD_DIM
    off = {}
    pos = 0
    for name, width in (("dq", W), ("iq", IDX_HEADS * IDX_DIM), ("dk", HEAD_DIM), ("dv", HEAD_DIM),
                        ("misc", LANES), ("ff", LANES)):
        off[name] = pos
        pos += width
    total = -(-pos // 512) * 512
    return fh, W, off, total


def _post_body(pa_ref, pb_ref, pos_ref, fb_ref, fqg_ref, fkg_ref, dqg_ref, dkg_ref, inv1_ref, sgn1_ref, inv2_ref,
               sgn2_ref, hbf_ref, hbd_ref,
               fq_ref, fk_ref, fv_ref, dq_ref, dk_ref, dv_ref, iq_ref, ik_ref, iw_ref, carry_ref,
               *, fh, off, tm):
    @pl.when(pl.program_id(0) == 0)
    def _():
        carry_ref[...] = jnp.zeros_like(carry_ref)

    W = fh * HEAD_DIM
    pos = pos_ref[...].astype(F32)
    ang1 = pos * inv1_ref[...]
    cos1, sin1 = jnp.cos(ang1), jnp.sin(ang1) * sgn1_ref[...]
    ang2 = pos * inv2_ref[...]
    cos2, sin2 = jnp.cos(ang2), jnp.sin(ang2) * sgn2_ref[...]
    lane = lax.broadcasted_iota(I32, (tm, LANES), 1)
    lo_half = (lane & (IDX_DIM - 1)) < (IDX_DIM // 2)
    one_hot0 = jnp.where(lane == 0, 1.0, 0.0)

    def head(seg, h):
        s = seg * W + h * LANES
        return pa_ref[:, s:s + LANES]

    def tail(name, j=0):
        s = off[name] + j * LANES
        return pb_ref[:, s:s + LANES]

    def norm(xh, g_ref):
        return xh * lax.rsqrt(jnp.mean(xh * xh, axis=-1, keepdims=True) + NORM_EPS) * g_ref[...]

    def rope_full(xh):
        return xh * cos1 + pltpu.roll(xh, HEAD_DIM // 2, 1) * sin1

    def rope_idx(xh):
        r = jnp.where(lo_half, pltpu.roll(xh, LANES - IDX_DIM // 2, 1), pltpu.roll(xh, IDX_DIM // 2, 1))
        return xh * cos2 + r * sin2

    z = tail("ff") + fb_ref[...]
    c = jnp.minimum(z, 0.0) - jnp.log1p(jnp.exp(-jnp.abs(z)))
    row = lax.broadcasted_iota(I32, (tm, LANES), 0)
    s = 1
    while s < tm:
        c = c + jnp.where(row >= s, pltpu.roll(c, s, 0), 0.0)
        s *= 2
    c = c + carry_ref[...]
    carry_ref[...] = c[tm - 1:tm, :]
    nc = (-LOG2E) * c

    for h in range(fh):
        hs = slice(h * WIDE, h * WIDE + HEAD_DIM)
        xs = slice(h * WIDE + HEAD_DIM, (h + 1) * WIDE)
        fq_ref[:, hs] = (norm(head(0, h), fqg_ref) * QSCALE).astype(BF16)
        fk_ref[:, hs] = norm(head(1, h), fkg_ref).astype(BF16)
        fv_ref[:, hs] = head(2, h).astype(BF16)
        b = jnp.broadcast_to(nc[:, h:h + 1], (tm, LANES))
        hi = b.astype(BF16).astype(F32)
        mid = (b - hi).astype(BF16).astype(F32)
        lo = b - hi - mid
        k_aug = jnp.where(lane == 0, hi, jnp.where(lane == 1, mid, jnp.where(lane == 2, lo,
                                                                             jnp.where(lane == 3, 1.0, 0.0))))
        q_aug = jnp.where(lane < 3, 1.0, jnp.where(lane == 3, -b - hbf_ref[...], 0.0))
        fk_ref[:, xs] = k_aug.astype(BF16)
        fq_ref[:, xs] = q_aug.astype(BF16)
        fv_ref[:, xs] = one_hot0.astype(BF16)
        dq_ref[:, h * LANES:(h + 1) * LANES] = (rope_full(norm(tail("dq", h), dqg_ref)) * QSCALE).astype(BF16)
    dk_ref[:, :HEAD_DIM] = rope_full(norm(tail("dk"), dkg_ref)).astype(BF16)
    dk_ref[:, HEAD_DIM:] = (one_hot0 * (-hbd_ref[...])).astype(BF16)
    dv_ref[:, :HEAD_DIM] = tail("dv").astype(BF16)
    dv_ref[:, HEAD_DIM:] = one_hot0.astype(BF16)

    for j in range(IDX_HEADS // 2):
        blk = rope_idx(tail("iq", j)) * (IDX_DIM ** -0.5)
        iq_ref[2 * j] = blk[:, :IDX_DIM].astype(BF16)
        iq_ref[2 * j + 1] = blk[:, IDX_DIM:].astype(BF16)
    misc = tail("misc")
    ik_ref[...] = rope_idx(misc)[:, :IDX_DIM].astype(BF16)
    iw_ref[...] = misc[:, IDX_DIM:IDX_DIM + IDX_HEADS] * (IDX_HEADS ** -0.5)


def _post(proj_a, proj_b, positions, fb, fqg, fkg, dqg, dkg, half_span_f, half_span_d, D, tm=128):
    S, NB = proj_b.shape
    fh, W, off, total = _tail_layout(D)
    assert total == NB and proj_a.shape == (S, 3 * W)
    tm = min(tm, S)

    def pad_lanes(v):
        return jnp.zeros((1, LANES), F32).at[0, :v.shape[0]].set(v.astype(F32))

    half = HEAD_DIM // 2
    inv_full = ROPE_THETA ** (-jnp.arange(0, HEAD_DIM, 2, dtype=F32) / HEAD_DIM)
    inv1 = jnp.concatenate([inv_full, inv_full]).reshape(1, LANES)
    sgn1 = jnp.concatenate([-jnp.ones(half, F32), jnp.ones(half, F32)]).reshape(1, LANES)
    inv_idx = ROPE_THETA ** (-jnp.arange(0, IDX_DIM, 2, dtype=F32) / IDX_DIM)
    inv2 = jnp.tile(inv_idx, LANES // (IDX_DIM // 2)).reshape(1, LANES)
    q = IDX_DIM // 2
    sgn2 = jnp.tile(jnp.concatenate([-jnp.ones(q, F32), jnp.ones(q, F32)]), LANES // IDX_DIM).reshape(1, LANES)

    row = lambda w: pl.BlockSpec((tm, w), lambda i: (i, 0))
    const = pl.BlockSpec((1, LANES), lambda i: (0, 0))
    wide = jax.ShapeDtypeStruct((S, fh * WIDE), BF16)
    out_shape = (
        wide, wide, wide, jax.ShapeDtypeStruct((S, W), BF16), jax.ShapeDtypeStruct((S, WIDE), BF16),
        jax.ShapeDtypeStruct((S, WIDE), BF16), jax.ShapeDtypeStruct((IDX_HEADS, S, IDX_DIM), BF16),
        jax.ShapeDtypeStruct((S, IDX_DIM), BF16), jax.ShapeDtypeStruct((S, IDX_HEADS), F32),
    )
    out_specs = (
        row(fh * WIDE), row(fh * WIDE), row(fh * WIDE), row(W), row(WIDE), row(WIDE),
        pl.BlockSpec((IDX_HEADS, tm, IDX_DIM), lambda i: (0, i, 0)), row(IDX_DIM), row(IDX_HEADS),
    )
    return pl.pallas_call(
        functools.partial(_post_body, fh=fh, off=off, tm=tm),
        grid=(S // tm,),
        in_specs=[row(3 * W), row(NB), row(1)] + [const] * 11,
        out_specs=out_specs,
        out_shape=out_shape,
        scratch_shapes=[pltpu.VMEM((1, LANES), F32)],
        compiler_params=_params(40, "arbitrary"),
        name="post",
    )(proj_a, proj_b, positions.reshape(S, 1).astype(I32), pad_lanes(fb), fqg.reshape(1, LANES).astype(F32),
      fkg.reshape(1, LANES).astype(F32), dqg.reshape(1, LANES).astype(F32), dkg.reshape(1, LANES).astype(F32),
      inv1, sgn1, inv2, sgn2, jnp.full((1, LANES), half_span_f, F32), jnp.full((1, LANES), half_span_d, F32))


def _attend(s, v, acc_ref, m_ref, track_max):
    if not track_max:
        acc_ref[...] += jnp.dot(jnp.exp2(s).astype(BF16), v, preferred_element_type=F32)
        return
    m_prev = m_ref[...]
    m_new = jnp.maximum(m_prev, jnp.max(s, axis=-1, keepdims=True))
    m_safe = jnp.where(m_new == -jnp.inf, 0.0, m_new)
    alpha = jnp.exp2(m_prev - m_safe)
    p = jnp.exp2(s - jnp.tile(m_safe, (1, s.shape[1] // LANES)))
    acc_ref[...] = jnp.tile(alpha, (1, WIDE // LANES)) * acc_ref[...] + jnp.dot(p.astype(BF16), v,
                                                                                 preferred_element_type=F32)
    m_ref[...] = m_new


def _normalised(acc):
    return acc[:, :HEAD_DIM] / acc[:, HEAD_DIM:HEAD_DIM + 1]


def _logit_span(gq, gk):
    half = LOGIT_BOUND_PER_GAIN * jnp.max(jnp.abs(gq.astype(F32))) * jnp.max(jnp.abs(gk.astype(F32)))
    return 2.0 * half <= SAFE_LOGIT_SPAN, half


def _fox_body(q_ref, k_ref, v_ref, o_ref, acc_ref, m_ref, *, tb, hp, track_max):
    qi = pl.program_id(1)
    acc_ref[...] = jnp.zeros_like(acc_ref)
    if track_max:
        m_ref[...] = jnp.full_like(m_ref, -jnp.inf)

    def step(ki, masked):
        ks = pl.multiple_of(ki * tb, tb)
        for h in range(hp):
            ws = slice(h * WIDE, (h + 1) * WIDE)
            s = lax.dot_general(q_ref[:, ws], k_ref[pl.ds(ks, tb), ws], NT_DIMS, preferred_element_type=F32)
            if masked:
                r = lax.broadcasted_iota(I32, (tb, tb), 0)
                c = lax.broadcasted_iota(I32, (tb, tb), 1)
                s = jnp.where(c <= r, s, -jnp.inf)
            _attend(s, v_ref[pl.ds(ks, tb), ws], acc_ref.at[h], m_ref.at[h], track_max)

    def loop_body(ki, carry):
        step(ki, False)
        return carry

    lax.fori_loop(0, qi, loop_body, 0)
    step(qi, True)
    for h in range(hp):
        o_ref[:, h * HEAD_DIM:(h + 1) * HEAD_DIM] = _normalised(acc_ref[h]).astype(o_ref.dtype)


def _fox_attention(fq, fk, fv, *, track_max, tb=512, hp=2):
    S = fq.shape[0]
    fh = fq.shape[1] // WIDE
    tb = min(tb, S)
    hp = min(hp, fh)
    return pl.pallas_call(
        functools.partial(_fox_body, tb=tb, hp=hp, track_max=track_max),
        grid=(fh // hp, S // tb),
        in_specs=[
            pl.BlockSpec((tb, hp * WIDE), lambda h, i: (i, h)),
            pl.BlockSpec((S, hp * WIDE), lambda h, i: (0, h)),
            pl.BlockSpec((S, hp * WIDE), lambda h, i: (0, h)),
        ],
        out_specs=pl.BlockSpec((tb, hp * HEAD_DIM), lambda h, i: (i, h)),
        out_shape=jax.ShapeDtypeStruct((S, fh * HEAD_DIM), BF16),
        scratch_shapes=[pltpu.VMEM((hp, tb, WIDE), F32), pltpu.VMEM((hp, tb, LANES), F32)],
        compiler_params=_params(56, "arbitrary", "arbitrary"),
        name="fox_attention",
    )(fq, fk, fv)


def _sortable_key(x):
    b = lax.bitcast_convert_type(x, I32)
    return b ^ ((b >> 31) & 0x7FFFFFFF)


def _dsa_body(iq_ref, ik_ref, iw_ref, dq_ref, dk_ref, dv_ref, o_ref,
              keys_ref, w_ref, q_ref, acc_ref, m_ref, *, tq, tk, topk, nh, track_max):
    qi = pl.program_id(0)
    groups = 2 if nh % 2 == 0 else 1
    ig, ag = IDX_HEADS // groups, nh // groups
    nkb = (qi * tq) // tk + 1
    row_g = qi * tq + lax.broadcasted_iota(I32, (tq, tk), 0)
    col_l = lax.broadcasted_iota(I32, (tq, tk), 1)

    for h in range(IDX_HEADS):
        w_ref[h * tq:(h + 1) * tq, :] = iw_ref[:, h:h + 1]

    def score_chunk(kb, carry):
        ks = pl.multiple_of(kb * tk, tk)
        ikb = ik_ref[pl.ds(ks, tk), :]
        sc = None
        for g in range(groups):
            rows = slice(g * ig * tq, (g + 1) * ig * tq)
            iq_g = iq_ref[g * ig:(g + 1) * ig].reshape(ig * tq, IDX_DIM)
            logits = lax.dot_general(iq_g, ikb, NT_DIMS, preferred_element_type=F32)
            part = jnp.sum((jnp.maximum(logits, 0.0) * w_ref[rows, :]).reshape(ig, tq, tk), axis=0)
            sc = part if sc is None else sc + part
        sc = jnp.where(ks + col_l <= row_g, sc, -jnp.inf)
        keys_ref[kb] = _sortable_key(sc)
        return carry

    lax.fori_loop(0, nkb, score_chunk, 0)

    def bit_step(i, thr):
        cand = thr + lax.shift_left(jnp.int32(1), 31 - i)

        def count_chunk(kb, cnt):
            hit = jnp.where(keys_ref[kb] >= cand, 1.0, 0.0)
            return cnt + sum(hit[:, j * LANES:(j + 1) * LANES] for j in range(tk // LANES))

        cnt = lax.fori_loop(0, nkb, count_chunk, jnp.zeros((tq, LANES), F32))
        cnt = jnp.sum(cnt, axis=-1, keepdims=True)
        return jnp.where(cnt >= float(topk), cand, thr)

    thr = lax.fori_loop(0, 32, bit_step, jnp.full((tq, 1), INT32_MIN, I32))
    thr = jnp.maximum(thr, NEG_INF_KEY + 1)

    lane = lax.broadcasted_iota(I32, (nh * tq, LANES), 1)
    q_ref[:, HEAD_DIM:] = jnp.where(lane == 0, 1.0, 0.0).astype(BF16)
    for h in range(nh):
        q_ref[h * tq:(h + 1) * tq, :HEAD_DIM] = dq_ref[:, h * HEAD_DIM:(h + 1) * HEAD_DIM]
    acc_ref[...] = jnp.zeros_like(acc_ref)
    if track_max:
        m_ref[...] = jnp.full_like(m_ref, -jnp.inf)

    def attn_chunk(kb, carry):
        ks = pl.multiple_of(kb * tk, tk)
        k = dk_ref[pl.ds(ks, tk), :]
        v = dv_ref[pl.ds(ks, tk), :]
        drop = jnp.where(keys_ref[kb] >= thr, 0.0, -jnp.inf)
        for g in range(groups):
            rows = pl.ds(g * ag * tq, ag * tq)
            s = lax.dot_general(q_ref[rows, :], k, NT_DIMS, preferred_element_type=F32)
            s = (s.reshape(ag, tq, tk) + drop[None]).reshape(ag * tq, tk)
            _attend(s, v, acc_ref.at[rows], m_ref.at[rows], track_max)
        return carry

    lax.fori_loop(0, nkb, attn_chunk, 0)
    out = _normalised(acc_ref[...])
    for h in range(nh):
        o_ref[:, h * HEAD_DIM:(h + 1) * HEAD_DIM] = out[h * tq:(h + 1) * tq, :].astype(o_ref.dtype)


def _dsa_attention(iq, ik, iw, dq, dk, dv, *, topk, track_max, tq=128, tk=512):
    S, W = dq.shape
    nh = W // HEAD_DIM
    tq, tk = min(tq, S), min(tk, S)
    nkb = S // tk
    full = lambda shape: pl.BlockSpec(shape, lambda i: (0,) * len(shape))
    return pl.pallas_call(
        functools.partial(_dsa_body, tq=tq, tk=tk, topk=topk, nh=nh, track_max=track_max),
        grid=(S // tq,),
        in_specs=[
            pl.BlockSpec((IDX_HEADS, tq, IDX_DIM), lambda i: (0, i, 0)),
            full((S, IDX_DIM)),
            pl.BlockSpec((tq, IDX_HEADS), lambda i: (i, 0)),
            pl.BlockSpec((tq, W), lambda i: (i, 0)),
            full((S, WIDE)),
            full((S, WIDE)),
        ],
        out_specs=pl.BlockSpec((tq, W), lambda i: (i, 0)),
        out_shape=jax.ShapeDtypeStruct((S, W), BF16),
        scratch_shapes=[
            pltpu.VMEM((nkb, tq, tk), I32),
            pltpu.VMEM((IDX_HEADS * tq, 1), F32),
            pltpu.VMEM((nh * tq, WIDE), BF16),
            pltpu.VMEM((nh * tq, WIDE), F32),
            pltpu.VMEM((nh * tq, LANES), F32),
        ],
        compiler_params=_params(58, "arbitrary"),
        name="dsa_attention",
    )(iq, ik, iw, dq, dk, dv)


ROUTE_GROUP = 8


def _top_rows(s, n):
    rows = []
    for _ in range(n):
        m = jnp.max(s, axis=0, keepdims=True)
        rows.append(m)
        s = jnp.where(s == m, -jnp.inf, s)
    return rows


def _route_body(qt_ref, kk_ref, s1_ref, e1_ref, s2_ref, e2_ref, tau_ref):
    ts = qt_ref.shape[1]
    ng = PEER_NKEYS // ROUTE_GROUP
    for h in range(PEER_HEADS):
        qh = qt_ref[h * LANES:(h + 1) * LANES, :]
        sc = jnp.dot(kk_ref[h], qh, preferred_element_type=F32, precision=lax.Precision.HIGHEST)
        s1, s2 = sc[:PEER_NKEYS], sc[PEER_NKEYS:]
        v1 = _top_rows(s1, PEER_TOPK)
        v2 = jnp.concatenate(_top_rows(s2, PEER_TOPK), axis=0)
        cand = [v1[a] + v2[:-(-(PEER_TOPK // (a + 1)) // 8) * 8] for a in range(PEER_TOPK)]
        best = _top_rows(jnp.concatenate(cand, axis=0), PEER_TOPK)
        z = sum(jnp.exp(b - best[0]) for b in best)
        s1_ref[:, h] = s1.reshape(ng, ROUTE_GROUP, ts)
        e1_ref[:, h] = (jnp.exp(s1 - v1[0]) / z).reshape(ng, ROUTE_GROUP, ts)
        s2_ref[h] = s2
        e2_ref[h] = jnp.exp(s2 - v2[0:1])
        tau_ref[h:h + 1, :] = best[PEER_TOPK - 1]


def _peer_route(qt, kk, ts=256):
    NQ, S = qt.shape
    ts = min(ts, S)
    ng = PEER_NKEYS // ROUTE_GROUP
    grouped = jax.ShapeDtypeStruct((ng, PEER_HEADS, ROUTE_GROUP, S), F32)
    whole = jax.ShapeDtypeStruct((PEER_HEADS, PEER_NKEYS, S), F32)
    gspec = pl.BlockSpec((ng, PEER_HEADS, ROUTE_GROUP, ts), lambda i: (0, 0, 0, i))
    wspec = pl.BlockSpec((PEER_HEADS, PEER_NKEYS, ts), lambda i: (0, 0, i))
    return pl.pallas_call(
        _route_body,
        grid=(S // ts,),
        in_specs=[pl.BlockSpec((NQ, ts), lambda i: (0, i)),
                  pl.BlockSpec((PEER_HEADS, 2 * PEER_NKEYS, LANES), lambda i: (0, 0, 0))],
        out_specs=(gspec, gspec, wspec, wspec, pl.BlockSpec((PEER_HEADS, ts), lambda i: (0, i))),
        out_shape=(grouped, grouped, whole, whole, jax.ShapeDtypeStruct((PEER_HEADS, S), F32)),
        compiler_params=_params(40, "arbitrary"),
        name="peer_route",
    )(qt, kk)


def _experts_body(x_ref, u_ref, v_ref, s1_ref, e1_ref, s2_ref, e2_ref, tau_ref, res_ref, o_ref, *, nsub):
    eb = pl.program_id(1)

    @pl.when(eb == 0)
    def _():
        o_ref[...] = jnp.zeros_like(o_ref)

    rr = res_ref.shape[0]
    r0 = pl.multiple_of(eb * rr, rr)
    o_ref[pl.ds(r0, rr), :] += res_ref[...]

    first = (eb * nsub) % ROUTE_GROUP
    tb = x_ref.shape[0]
    ht = lax.dot_general(u_ref[...], x_ref[...], NT_DIMS, preferred_element_type=F32)
    parts = []
    for j in range(nsub):
        gate = jnp.zeros((PEER_NKEYS, tb), F32)
        for h in range(PEER_HEADS):
            s1 = s1_ref[h, pl.ds(first + j, 1), :]
            e1 = e1_ref[h, pl.ds(first + j, 1), :]
            routed = (s1 + s2_ref[h]) >= tau_ref[h:h + 1, :]
            gate = gate + jnp.where(routed, e2_ref[h] * e1, 0.0)
        hj = ht[j * PEER_NKEYS:(j + 1) * PEER_NKEYS, :]
        act = 0.5 * hj * (1.0 + lax.erf(hj * (2.0 ** -0.5)))
        parts.append(gate * act)
    a = jnp.concatenate(parts, axis=0).T.astype(BF16)
    o_ref[...] += jnp.dot(a, v_ref[...], preferred_element_type=F32)


def _peer_experts(h2, u, v, s1, e1, s2, e2, tau, res, tb=512, te=512):
    S, D = h2.shape
    E = u.shape[0]
    tb = min(tb, S)
    nsub = te // PEER_NKEYS
    ne = E // te
    rr = tb // ne
    assert ROUTE_GROUP % nsub == 0 and tb % ne == 0 and rr % 8 == 0
    grouped = pl.BlockSpec((None, PEER_HEADS, ROUTE_GROUP, tb), lambda i, j: ((j * nsub) // ROUTE_GROUP, 0, 0, i))
    whole = pl.BlockSpec((PEER_HEADS, PEER_NKEYS, tb), lambda i, j: (0, 0, i))
    return pl.pallas_call(
        functools.partial(_experts_body, nsub=nsub),
        grid=(S // tb, ne),
        in_specs=[
            pl.BlockSpec((tb, D), lambda i, j: (i, 0)),
            pl.BlockSpec((te, D), lambda i, j: (j, 0)),
            pl.BlockSpec((te, D), lambda i, j: (j, 0)),
            grouped, grouped, whole, whole,
            pl.BlockSpec((PEER_HEADS, tb), lambda i, j: (0, i)),
            pl.BlockSpec((rr, D), lambda i, j: (i * ne + j, 0)),
        ],
        out_specs=pl.BlockSpec((tb, D), lambda i, j: (i, 0)),
        out_shape=jax.ShapeDtypeStruct((S, D), F32),
        compiler_params=_params(60, "arbitrary", "arbitrary"),
        name="peer_experts",
    )(h2, u, v, s1, e1, s2, e2, tau, res)


def _relayout_w_tail(w_in, D):
    fh, W, off, total = _tail_layout(D)
    sizes = (fh, W, HEAD_DIM, HEAD_DIM, IDX_HEADS * IDX_DIM, IDX_DIM, IDX_HEADS)
    pts = (3 * W + np.cumsum(sizes)[:-1]).tolist()
    _, ff, dq, dk, dv, iq, ik, iw = jnp.split(w_in, [3 * W] + pts, axis=1)
    zeros = lambda n: jnp.zeros((D, n), w_in.dtype)
    misc = jnp.concatenate([ik, iw, zeros(LANES - IDX_DIM - IDX_HEADS)], axis=1)
    ffp = jnp.concatenate([ff, zeros(LANES - fh)], axis=1)
    cols = [dq, iq, dk, dv, misc, ffp]
    used = sum(c.shape[1] for c in cols)
    if total > used:
        cols.append(zeros(total - used))
    return jnp.concatenate(cols, axis=1).astype(BF16)


def _layer(x, positions, ln1_g, w_in, fox_forget_b, fox_qn_g, fox_kn_g, dsa_qn_g, dsa_kn_g,
           w_o, ln2_g, peer_wq, peer_keys1, peer_keys2, peer_u, peer_v):
    S, D = x.shape
    W = D // 2
    topk = min(DSA_TOPK_MAX, S // 4)

    h1 = _rmsnorm(x, ln1_g, BF16)
    proj_a = _matmul(h1, w_in, n_cols=3 * W, name="in_proj_fox")
    proj_b = _matmul(h1, _relayout_w_tail(w_in, D), name="in_proj_dsa")
    fox_safe, fox_half = _logit_span(fox_qn_g, fox_kn_g)
    dsa_safe, dsa_half = _logit_span(dsa_qn_g, dsa_kn_g)
    fq, fk, fv, dq, dk, dv, iq, ik, iw = _post(proj_a, proj_b, positions, fox_forget_b, fox_qn_g, fox_kn_g,
                                               dsa_qn_g, dsa_kn_g, fox_half, dsa_half, D)
    fox_out = lax.cond(fox_safe, functools.partial(_fox_attention, track_max=False),
                       functools.partial(_fox_attention, track_max=True), fq, fk, fv)
    dsa_out = lax.cond(dsa_safe, functools.partial(_dsa_attention, topk=topk, track_max=False),
                       functools.partial(_dsa_attention, topk=topk, track_max=True), iq, ik, iw, dq, dk, dv)
    x1 = _out_proj(fox_out, dsa_out, w_o, x)

    h2 = _rmsnorm(x1, ln2_g, BF16)
    qt = _matmul(peer_wq.T.astype(BF16), h2, nt=True, name="peer_query")
    zk = jnp.zeros_like(peer_keys1)
    kk = jnp.concatenate([jnp.concatenate([peer_keys1, zk], axis=2),
                          jnp.concatenate([zk, peer_keys2], axis=2)], axis=1).astype(F32)
    s1, e1, s2, e2, tau = _peer_route(qt, kk)
    return _peer_experts(h2, peer_u.astype(BF16), peer_v.astype(BF16), s1, e1, s2, e2, tau, x1)


def kernel(x, positions, ln1_g, w_in, fox_forget_b, fox_qn_g, fox_kn_g, dsa_qn_g, dsa_kn_g, w_o, ln2_g, peer_wq,
           peer_keys1, peer_keys2, peer_u, peer_v):
    B = x.shape[0]
    depth = w_in.shape[0]
    outs = []
    for b in range(B):
        xb = x[b]
        for l in range(depth):
            xb = _layer(xb, positions[b], ln1_g[l], w_in[l], fox_forget_b[l], fox_qn_g[l], fox_kn_g[l],
                        dsa_qn_g[l], dsa_kn_g[l], w_o[l], ln2_g[l], peer_wq[l], peer_keys1[l], peer_keys2[l],
                        peer_u[l], peer_v[l])
        outs.append(xb)
    return outs[0][None] if B == 1 else jnp.stack(outs, axis=0)
```

```python
import functools

import numpy as np
import jax
import jax.numpy as jnp
from jax import lax
from jax.experimental import pallas as pl
from jax.experimental.pallas import tpu as pltpu

F32 = jnp.float32
BF16 = jnp.bfloat16
I32 = jnp.int32

HEAD_DIM = 128
IDX_HEADS = 16
IDX_DIM = 64
DSA_TOPK_MAX = 256
ROPE_THETA = 10000.0
NORM_EPS = 1e-6
PEER_HEADS = 8
PEER_NKEYS = 128
PEER_SUBDIM = 64
PEER_TOPK = 16
LOG2E = 1.4426950408889634
LANES = 128
WIDE = 2 * HEAD_DIM
QSCALE = (HEAD_DIM ** -0.5) * LOG2E

LOGIT_BOUND_PER_GAIN = (HEAD_DIM ** 0.5) * LOG2E
SAFE_LOGIT_SPAN = 60.0

NEG_INF_KEY = -2139095041
INT32_MIN = -2147483648

NT_DIMS = (((1,), (1,)), ((), ()))
NN_DIMS = (((1,), (0,)), ((), ()))


def _params(vmem_mb, *sem):
    return pltpu.CompilerParams(dimension_semantics=sem, vmem_limit_bytes=vmem_mb * 1024 * 1024)


def _as_bf16(x):
    return x if x.dtype == BF16 else x.astype(BF16)


def _rmsnorm_body(x_ref, g_ref, o_ref):
    x = x_ref[...].astype(F32)
    y = x * lax.rsqrt(jnp.mean(x * x, axis=-1, keepdims=True) + NORM_EPS)
    o_ref[...] = (y * g_ref[...]).astype(o_ref.dtype)


def _rmsnorm(x2d, g, out_dtype, tm=256):
    S, D = x2d.shape
    tm = min(tm, S)
    return pl.pallas_call(
        _rmsnorm_body,
        grid=(S // tm,),
        in_specs=[pl.BlockSpec((tm, D), lambda i: (i, 0)), pl.BlockSpec((1, D), lambda i: (0, 0))],
        out_specs=pl.BlockSpec((tm, D), lambda i: (i, 0)),
        out_shape=jax.ShapeDtypeStruct((S, D), out_dtype),
        compiler_params=_params(40, "arbitrary"),
        name="rmsnorm",
    )(x2d, g.reshape(1, D).astype(F32))


def _matmul_body(a_ref, b_ref, o_ref, *, nt):
    acc = lax.dot_general(a_ref[...], _as_bf16(b_ref[...]), NT_DIMS if nt else NN_DIMS, preferred_element_type=F32)
    o_ref[...] = acc.astype(o_ref.dtype)


def _matmul(a, b, *, nt=False, n_cols=None, out_dtype=F32, tm=1024, tn=512, name="matmul"):
    M, K = a.shape
    N = n_cols if n_cols is not None else (b.shape[0] if nt else b.shape[1])
    tm = min(tm, M)
    tn = next(t for t in range(min(tn, N), 0, -LANES) if N % t == 0)
    assert M % tm == 0 and tn % LANES == 0
    return pl.pallas_call(
        functools.partial(_matmul_body, nt=nt),
        grid=(M // tm, N // tn),
        in_specs=[
            pl.BlockSpec((tm, K), lambda i, j: (i, 0)),
            pl.BlockSpec((tn, K), lambda i, j: (j, 0)) if nt else pl.BlockSpec((K, tn), lambda i, j: (0, j)),
        ],
        out_specs=pl.BlockSpec((tm, tn), lambda i, j: (i, j)),
        out_shape=jax.ShapeDtypeStruct((M, N), out_dtype),
        compiler_params=_params(48, "arbitrary", "arbitrary"),
        name=name,
    )(a, b)


def _out_proj_body(a1_ref, a2_ref, b1_ref, b2_ref, r_ref, o_ref):
    acc = jnp.dot(a1_ref[...], _as_bf16(b1_ref[...]), preferred_element_type=F32)
    acc = acc + jnp.dot(a2_ref[...], _as_bf16(b2_ref[...]), preferred_element_type=F32)
    o_ref[...] = acc + r_ref[...]


def _out_proj(a1, a2, w, res, tm=1024, tn=512):
    M, K1 = a1.shape
    N = w.shape[1]
    assert a2.shape == a1.shape and w.shape[0] == 2 * K1
    tm, tn = min(tm, M), min(tn, N)
    lhs = pl.BlockSpec((tm, K1), lambda i, j: (i, 0))
    return pl.pallas_call(
        _out_proj_body,
        grid=(M // tm, N // tn),
        in_specs=[lhs, lhs, pl.BlockSpec((K1, tn), lambda i, j: (0, j)), pl.BlockSpec((K1, tn), lambda i, j: (1, j)),
                  pl.BlockSpec((tm, tn), lambda i, j: (i, j))],
        out_specs=pl.BlockSpec((tm, tn), lambda i, j: (i, j)),
        out_shape=jax.ShapeDtypeStruct((M, N), F32),
        compiler_params=_params(48, "arbitrary", "arbitrary"),
        name="out_proj",
    )(a1, a2, w, w, res)


def _tail_layout(D):
    fh = D // (2 * HEAD_DIM)
    W = fh * HEAD_DIM
    off = {}
    pos = 0
    for name, width in (("dq", W), ("iq", IDX_HEADS * IDX_DIM), ("dk", HEAD_DIM), ("dv", HEAD_DIM),
                        ("misc", LANES), ("ff", LANES)):
        off[name] = pos
        pos += width
    total = -(-pos // 512) * 512
    return fh, W, off, total


def _post_body(pa_ref, pb_ref, pos_ref, fb_ref, fqg_ref, fkg_ref, dqg_ref, dkg_ref, inv1_ref, sgn1_ref, inv2_ref,
               sgn2_ref, hbf_ref, hbd_ref,
               fq_ref, fk_ref, fv_ref, dq_ref, dk_ref, dv_ref, iq_ref, ik_ref, iw_ref, carry_ref,
               *, fh, off, tm):
    @pl.when(pl.program_id(0) == 0)
    def _():
        carry_ref[...] = jnp.zeros_like(carry_ref)

    W = fh * HEAD_DIM
    pos = pos_ref[...].astype(F32)
    ang1 = pos * inv1_ref[...]
    cos1, sin1 = jnp.cos(ang1), jnp.sin(ang1) * sgn1_ref[...]
    ang2 = pos * inv2_ref[...]
    cos2, sin2 = jnp.cos(ang2), jnp.sin(ang2) * sgn2_ref[...]
    lane = lax.broadcasted_iota(I32, (tm, LANES), 1)
    lo_half = (lane & (IDX_DIM - 1)) < (IDX_DIM // 2)
    one_hot0 = jnp.where(lane == 0, 1.0, 0.0)

    def head(seg, h):
        s = seg * W + h * LANES
        return pa_ref[:, s:s + LANES]

    def tail(name, j=0):
        s = off[name] + j * LANES
        return pb_ref[:, s:s + LANES]

    def norm(xh, g_ref):
        return xh * lax.rsqrt(jnp.mean(xh * xh, axis=-1, keepdims=True) + NORM_EPS) * g_ref[...]

    def rope_full(xh):
        return xh * cos1 + pltpu.roll(xh, HEAD_DIM // 2, 1) * sin1

    def rope_idx(xh):
        r = jnp.where(lo_half, pltpu.roll(xh, LANES - IDX_DIM // 2, 1), pltpu.roll(xh, IDX_DIM // 2, 1))
        return xh * cos2 + r * sin2

    z = tail("ff") + fb_ref[...]
    c = jnp.minimum(z, 0.0) - jnp.log1p(jnp.exp(-jnp.abs(z)))
    row = lax.broadcasted_iota(I32, (tm, LANES), 0)
    s = 1
    while s < tm:
        c = c + jnp.where(row >= s, pltpu.roll(c, s, 0), 0.0)
        s *= 2
    c = c + carry_ref[...]
    carry_ref[...] = c[tm - 1:tm, :]
    nc = (-LOG2E) * c

    for h in range(fh):
        hs = slice(h * WIDE, h * WIDE + HEAD_DIM)
        xs = slice(h * WIDE + HEAD_DIM, (h + 1) * WIDE)
        fq_ref[:, hs] = (norm(head(0, h), fqg_ref) * QSCALE).astype(BF16)
        fk_ref[:, hs] = norm(head(1, h), fkg_ref).astype(BF16)
        fv_ref[:, hs] = head(2, h).astype(BF16)
        b = jnp.broadcast_to(nc[:, h:h + 1], (tm, LANES))
        hi = b.astype(BF16).astype(F32)
        mid = (b - hi).astype(BF16).astype(F32)
        lo = b - hi - mid
        k_aug = jnp.where(lane == 0, hi, jnp.where(lane == 1, mid, jnp.where(lane == 2, lo,
                                                                             jnp.where(lane == 3, 1.0, 0.0))))
        q_aug = jnp.where(lane < 3, 1.0, jnp.where(lane == 3, -b - hbf_ref[...], 0.0))
        fk_ref[:, xs] = k_aug.astype(BF16)
        fq_ref[:, xs] = q_aug.astype(BF16)
        fv_ref[:, xs] = one_hot0.astype(BF16)
        dq_ref[:, h * LANES:(h + 1) * LANES] = (rope_full(norm(tail("dq", h), dqg_ref)) * QSCALE).astype(BF16)
    dk_ref[:, :HEAD_DIM] = rope_full(norm(tail("dk"), dkg_ref)).astype(BF16)
    dk_ref[:, HEAD_DIM:] = (one_hot0 * (-hbd_ref[...])).astype(BF16)
    dv_ref[:, :HEAD_DIM] = tail("dv").astype(BF16)
    dv_ref[:, HEAD_DIM:] = one_hot0.astype(BF16)

    for j in range(IDX_HEADS // 2):
        blk = rope_idx(tail("iq", j)) * (IDX_DIM ** -0.5)
        iq_ref[2 * j] = blk[:, :IDX_DIM].astype(BF16)
        iq_ref[2 * j + 1] = blk[:, IDX_DIM:].astype(BF16)
    misc = tail("misc")
    ik_ref[...] = rope_idx(misc)[:, :IDX_DIM].astype(BF16)
    iw_ref[...] = misc[:, IDX_DIM:IDX_DIM + IDX_HEADS] * (IDX_HEADS ** -0.5)


def _post(proj_a, proj_b, positions, fb, fqg, fkg, dqg, dkg, half_span_f, half_span_d, D, tm=128):
    S, NB = proj_b.shape
    fh, W, off, total = _tail_layout(D)
    assert total == NB and proj_a.shape == (S, 3 * W)
    tm = min(tm, S)

    def pad_lanes(v):
        return jnp.zeros((1, LANES), F32).at[0, :v.shape[0]].set(v.astype(F32))

    half = HEAD_DIM // 2
    inv_full = ROPE_THETA ** (-jnp.arange(0, HEAD_DIM, 2, dtype=F32) / HEAD_DIM)
    inv1 = jnp.concatenate([inv_full, inv_full]).reshape(1, LANES)
    sgn1 = jnp.concatenate([-jnp.ones(half, F32), jnp.ones(half, F32)]).reshape(1, LANES)
    inv_idx = ROPE_THETA ** (-jnp.arange(0, IDX_DIM, 2, dtype=F32) / IDX_DIM)
    inv2 = jnp.tile(inv_idx, LANES // (IDX_DIM // 2)).reshape(1, LANES)
    q = IDX_DIM // 2
    sgn2 = jnp.tile(jnp.concatenate([-jnp.ones(q, F32), jnp.ones(q, F32)]), LANES // IDX_DIM).reshape(1, LANES)

    row = lambda w: pl.BlockSpec((tm, w), lambda i: (i, 0))
    const = pl.BlockSpec((1, LANES), lambda i: (0, 0))
    wide = jax.ShapeDtypeStruct((S, fh * WIDE), BF16)
    out_shape = (
        wide, wide, wide, jax.ShapeDtypeStruct((S, W), BF16), jax.ShapeDtypeStruct((S, WIDE), BF16),
        jax.ShapeDtypeStruct((S, WIDE), BF16), jax.ShapeDtypeStruct((IDX_HEADS, S, IDX_DIM), BF16),
        jax.ShapeDtypeStruct((S, IDX_DIM), BF16), jax.ShapeDtypeStruct((S, IDX_HEADS), F32),
    )
    out_specs = (
        row(fh * WIDE), row(fh * WIDE), row(fh * WIDE), row(W), row(WIDE), row(WIDE),
        pl.BlockSpec((IDX_HEADS, tm, IDX_DIM), lambda i: (0, i, 0)), row(IDX_DIM), row(IDX_HEADS),
    )
    return pl.pallas_call(
        functools.partial(_post_body, fh=fh, off=off, tm=tm),
        grid=(S // tm,),
        in_specs=[row(3 * W), row(NB), row(1)] + [const] * 11,
        out_specs=out_specs,
        out_shape=out_shape,
        scratch_shapes=[pltpu.VMEM((1, LANES), F32)],
        compiler_params=_params(40, "arbitrary"),
        name="post",
    )(proj_a, proj_b, positions.reshape(S, 1).astype(I32), pad_lanes(fb), fqg.reshape(1, LANES).astype(F32),
      fkg.reshape(1, LANES).astype(F32), dqg.reshape(1, LANES).astype(F32), dkg.reshape(1, LANES).astype(F32),
      inv1, sgn1, inv2, sgn2, jnp.full((1, LANES), half_span_f, F32), jnp.full((1, LANES), half_span_d, F32))


def _attend(s, v, acc_ref, m_ref, track_max):
    if not track_max:
        acc_ref[...] += jnp.dot(jnp.exp2(s).astype(BF16), v, preferred_element_type=F32)
        return
    m_prev = m_ref[...]
    m_new = jnp.maximum(m_prev, jnp.max(s, axis=-1, keepdims=True))
    m_safe = jnp.where(m_new == -jnp.inf, 0.0, m_new)
    alpha = jnp.exp2(m_prev - m_safe)
    p = jnp.exp2(s - jnp.tile(m_safe, (1, s.shape[1] // LANES)))
    acc_ref[...] = jnp.tile(alpha, (1, WIDE // LANES)) * acc_ref[...] + jnp.dot(p.astype(BF16), v,
                                                                                 preferred_element_type=F32)
    m_ref[...] = m_new


def _normalised(acc):
    return acc[:, :HEAD_DIM] / acc[:, HEAD_DIM:HEAD_DIM + 1]


def _logit_span(gq, gk):
    half = LOGIT_BOUND_PER_GAIN * jnp.max(jnp.abs(gq.astype(F32))) * jnp.max(jnp.abs(gk.astype(F32)))
    return 2.0 * half <= SAFE_LOGIT_SPAN, half


def _fox_body(q_ref, k_ref, v_ref, o_ref, acc_ref, m_ref, *, tb, hp, track_max):
    qi = pl.program_id(1)
    acc_ref[...] = jnp.zeros_like(acc_ref)
    if track_max:
        m_ref[...] = jnp.full_like(m_ref, -jnp.inf)

    def step(ki, masked):
        ks = pl.multiple_of(ki * tb, tb)
        for h in range(hp):
            ws = slice(h * WIDE, (h + 1) * WIDE)
            s = lax.dot_general(q_ref[:, ws], k_ref[pl.ds(ks, tb), ws], NT_DIMS, preferred_element_type=F32)
            if masked:
                r = lax.broadcasted_iota(I32, (tb, tb), 0)
                c = lax.broadcasted_iota(I32, (tb, tb), 1)
                s = jnp.where(c <= r, s, -jnp.inf)
            _attend(s, v_ref[pl.ds(ks, tb), ws], acc_ref.at[h], m_ref.at[h], track_max)

    def loop_body(ki, carry):
        step(ki, False)
        return carry

    lax.fori_loop(0, qi, loop_body, 0)
    step(qi, True)
    for h in range(hp):
        o_ref[:, h * HEAD_DIM:(h + 1) * HEAD_DIM] = _normalised(acc_ref[h]).astype(o_ref.dtype)


def _fox_attention(fq, fk, fv, *, track_max, tb=512, hp=4):
    S = fq.shape[0]
    fh = fq.shape[1] // WIDE
    tb = min(tb, S)
    hp = min(hp, fh)
    once = pl.Buffered(1)
    return pl.pallas_call(
        functools.partial(_fox_body, tb=tb, hp=hp, track_max=track_max),
        grid=(fh // hp, S // tb),
        in_specs=[
            pl.BlockSpec((tb, hp * WIDE), lambda h, i: (i, h)),
            pl.BlockSpec((S, hp * WIDE), lambda h, i: (0, h), pipeline_mode=once),
            pl.BlockSpec((S, hp * WIDE), lambda h, i: (0, h), pipeline_mode=once),
        ],
        out_specs=pl.BlockSpec((tb, hp * HEAD_DIM), lambda h, i: (i, h)),
        out_shape=jax.ShapeDtypeStruct((S, fh * HEAD_DIM), BF16),
        scratch_shapes=[pltpu.VMEM((hp, tb, WIDE), F32), pltpu.VMEM((hp, tb, LANES), F32)],
        compiler_params=_params(56, "arbitrary", "arbitrary"),
        name="fox_attention",
    )(fq, fk, fv)


def _sortable_key(x):
    b = lax.bitcast_convert_type(x, I32)
    return b ^ ((b >> 31) & 0x7FFFFFFF)


def _dsa_body(iq_ref, ik_ref, iw_ref, dq_ref, dk_ref, dv_ref, o_ref,
              keys_ref, w_ref, q_ref, acc_ref, m_ref, *, tq, tk, topk, nh, track_max):
    qi = pl.program_id(0)
    groups = max([1] + [g for g in (2, 4, 8) if nh % g == 0 and IDX_HEADS % g == 0 and g * GROUP_ROWS <= nh * tq])
    ig, ag = IDX_HEADS // groups, nh // groups
    nkb = (qi * tq) // tk + 1
    row_g = qi * tq + lax.broadcasted_iota(I32, (tq, tk), 0)
    col_l = lax.broadcasted_iota(I32, (tq, tk), 1)

    for h in range(IDX_HEADS):
        w_ref[h * tq:(h + 1) * tq, :] = iw_ref[:, h:h + 1]

    def score_chunk(kb, carry):
        ks = pl.multiple_of(kb * tk, tk)
        ikb = ik_ref[pl.ds(ks, tk), :]
        sc = None
        for g in range(groups):
            rows = slice(g * ig * tq, (g + 1) * ig * tq)
            iq_g = iq_ref[g * ig:(g + 1) * ig].reshape(ig * tq, IDX_DIM)
            logits = lax.dot_general(iq_g, ikb, NT_DIMS, preferred_element_type=F32)
            part = jnp.sum((jnp.maximum(logits, 0.0) * w_ref[rows, :]).reshape(ig, tq, tk), axis=0)
            sc = part if sc is None else sc + part
        sc = jnp.where(ks + col_l <= row_g, sc, -jnp.inf)
        keys_ref[kb] = _sortable_key(sc)
        return carry

    lax.fori_loop(0, nkb, score_chunk, 0)

    def bit_step(i, thr):
        cand = thr + lax.shift_left(jnp.int32(1), 31 - i)

        def count_chunk(kb, cnt):
            hit = jnp.where(keys_ref[kb] >= cand, 1.0, 0.0)
            return cnt + sum(hit[:, j * LANES:(j + 1) * LANES] for j in range(tk // LANES))

        cnt = lax.fori_loop(0, nkb, count_chunk, jnp.zeros((tq, LANES), F32))
        cnt = jnp.sum(cnt, axis=-1, keepdims=True)
        return jnp.where(cnt >= float(topk), cand, thr)

    thr = lax.fori_loop(0, 32, bit_step, jnp.full((tq, 1), INT32_MIN, I32))
    thr = jnp.maximum(thr, NEG_INF_KEY + 1)

    lane = lax.broadcasted_iota(I32, (nh * tq, LANES), 1)
    q_ref[:, HEAD_DIM:] = jnp.where(lane == 0, 1.0, 0.0).astype(BF16)
    for h in range(nh):
        q_ref[h * tq:(h + 1) * tq, :HEAD_DIM] = dq_ref[:, h * HEAD_DIM:(h + 1) * HEAD_DIM]
    acc_ref[...] = jnp.zeros_like(acc_ref)
    if track_max:
        m_ref[...] = jnp.full_like(m_ref, -jnp.inf)

    def attn_chunk(kb, carry):
        ks = pl.multiple_of(kb * tk, tk)
        k = dk_ref[pl.ds(ks, tk), :]
        v = dv_ref[pl.ds(ks, tk), :]
        drop = jnp.where(keys_ref[kb] >= thr, 0.0, -jnp.inf)
        for g in range(groups):
            rows = pl.ds(g * ag * tq, ag * tq)
            s = lax.dot_general(q_ref[rows, :], k, NT_DIMS, preferred_element_type=F32)
            s = (s.reshape(ag, tq, tk) + drop[None]).reshape(ag * tq, tk)
            _attend(s, v, acc_ref.at[rows], m_ref.at[rows], track_max)
        return carry

    lax.fori_loop(0, nkb, attn_chunk, 0)
    out = _normalised(acc_ref[...])
    for h in range(nh):
        o_ref[:, h * HEAD_DIM:(h + 1) * HEAD_DIM] = out[h * tq:(h + 1) * tq, :].astype(o_ref.dtype)


GROUP_ROWS = 1024


def _dsa_attention(iq, ik, iw, dq, dk, dv, *, topk, track_max, tq=256, tk=512):
    S, W = dq.shape
    nh = W // HEAD_DIM
    tq, tk = min(tq, S), min(tk, S)
    nkb = S // tk
    full = lambda shape: pl.BlockSpec(shape, lambda i: (0,) * len(shape))
    return pl.pallas_call(
        functools.partial(_dsa_body, tq=tq, tk=tk, topk=topk, nh=nh, track_max=track_max),
        grid=(S // tq,),
        in_specs=[
            pl.BlockSpec((IDX_HEADS, tq, IDX_DIM), lambda i: (0, i, 0)),
            full((S, IDX_DIM)),
            pl.BlockSpec((tq, IDX_HEADS), lambda i: (i, 0)),
            pl.BlockSpec((tq, W), lambda i: (i, 0)),
            full((S, WIDE)),
            full((S, WIDE)),
        ],
        out_specs=pl.BlockSpec((tq, W), lambda i: (i, 0)),
        out_shape=jax.ShapeDtypeStruct((S, W), BF16),
        scratch_shapes=[
            pltpu.VMEM((nkb, tq, tk), I32),
            pltpu.VMEM((IDX_HEADS * tq, 1), F32),
            pltpu.VMEM((nh * tq, WIDE), BF16),
            pltpu.VMEM((nh * tq, WIDE), F32),
            pltpu.VMEM((nh * tq, LANES), F32),
        ],
        compiler_params=_params(58, "arbitrary"),
        name="dsa_attention",
    )(iq, ik, iw, dq, dk, dv)


ROUTE_GROUP = 8


def _top_rows(s, n):
    rows = []
    for _ in range(n):
        m = jnp.max(s, axis=0, keepdims=True)
        rows.append(m)
        s = jnp.where(s == m, -jnp.inf, s)
    return rows


def _route_body(qt_ref, kk_ref, s1_ref, e1_ref, s2_ref, e2_ref, tau_ref):
    ts = qt_ref.shape[1]
    ng = PEER_NKEYS // ROUTE_GROUP
    for h in range(PEER_HEADS):
        qh = qt_ref[h * LANES:(h + 1) * LANES, :]
        sc = jnp.dot(kk_ref[h], qh, preferred_element_type=F32, precision=lax.Precision.HIGHEST)
        s1, s2 = sc[:PEER_NKEYS], sc[PEER_NKEYS:]
        v1 = _top_rows(s1, PEER_TOPK)
        v2 = jnp.concatenate(_top_rows(s2, PEER_TOPK), axis=0)
        cand = [v1[a] + v2[:-(-(PEER_TOPK // (a + 1)) // 8) * 8] for a in range(PEER_TOPK)]
        best = _top_rows(jnp.concatenate(cand, axis=0), PEER_TOPK)
        z = sum(jnp.exp(b - best[0]) for b in best)
        s1_ref[:, h] = s1.reshape(ng, ROUTE_GROUP, ts)
        e1_ref[:, h] = (jnp.exp(s1 - v1[0]) / z).reshape(ng, ROUTE_GROUP, ts)
        s2_ref[h] = s2
        e2_ref[h] = jnp.exp(s2 - v2[0:1])
        tau_ref[h:h + 1, :] = best[PEER_TOPK - 1]


def _peer_route(qt, kk, ts=256):
    NQ, S = qt.shape
    ts = min(ts, S)
    ng = PEER_NKEYS // ROUTE_GROUP
    grouped = jax.ShapeDtypeStruct((ng, PEER_HEADS, ROUTE_GROUP, S), F32)
    whole = jax.ShapeDtypeStruct((PEER_HEADS, PEER_NKEYS, S), F32)
    gspec = pl.BlockSpec((ng, PEER_HEADS, ROUTE_GROUP, ts), lambda i: (0, 0, 0, i))
    wspec = pl.BlockSpec((PEER_HEADS, PEER_NKEYS, ts), lambda i: (0, 0, i))
    return pl.pallas_call(
        _route_body,
        grid=(S // ts,),
        in_specs=[pl.BlockSpec((NQ, ts), lambda i: (0, i)),
                  pl.BlockSpec((PEER_HEADS, 2 * PEER_NKEYS, LANES), lambda i: (0, 0, 0))],
        out_specs=(gspec, gspec, wspec, wspec, pl.BlockSpec((PEER_HEADS, ts), lambda i: (0, i))),
        out_shape=(grouped, grouped, whole, whole, jax.ShapeDtypeStruct((PEER_HEADS, S), F32)),
        compiler_params=_params(40, "arbitrary"),
        name="peer_route",
    )(qt, kk)


def _experts_body(x_ref, u_ref, v_ref, s1_ref, e1_ref, s2_ref, e2_ref, tau_ref, res_ref, o_ref, *, nsub):
    eb = pl.program_id(1)

    @pl.when(eb == 0)
    def _():
        o_ref[...] = jnp.zeros_like(o_ref)

    rr = res_ref.shape[0]
    r0 = pl.multiple_of(eb * rr, rr)
    o_ref[pl.ds(r0, rr), :] += res_ref[...]

    first = (eb * nsub) % ROUTE_GROUP
    tb = x_ref.shape[0]
    ht = lax.dot_general(u_ref[...], x_ref[...], NT_DIMS, preferred_element_type=F32)
    parts = []
    for j in range(nsub):
        gate = jnp.zeros((PEER_NKEYS, tb), F32)
        for h in range(PEER_HEADS):
            s1 = s1_ref[h, pl.ds(first + j, 1), :]
            e1 = e1_ref[h, pl.ds(first + j, 1), :]
            routed = (s1 + s2_ref[h]) >= tau_ref[h:h + 1, :]
            gate = gate + jnp.where(routed, e2_ref[h] * e1, 0.0)
        hj = ht[j * PEER_NKEYS:(j + 1) * PEER_NKEYS, :]
        act = 0.5 * hj * (1.0 + lax.erf(hj * (2.0 ** -0.5)))
        parts.append(gate * act)
    a = jnp.concatenate(parts, axis=0).T.astype(BF16)
    o_ref[...] += jnp.dot(a, v_ref[...], preferred_element_type=F32)


def _peer_experts(h2, u, v, s1, e1, s2, e2, tau, res, tb=512, te=1024):
    S, D = h2.shape
    E = u.shape[0]
    tb = min(tb, S)
    nsub = te // PEER_NKEYS
    ne = E // te
    rr = tb // ne
    assert ROUTE_GROUP % nsub == 0 and tb % ne == 0 and rr % 8 == 0
    grouped = pl.BlockSpec((None, PEER_HEADS, ROUTE_GROUP, tb), lambda i, j: ((j * nsub) // ROUTE_GROUP, 0, 0, i))
    once = pl.Buffered(1)
    whole = pl.BlockSpec((PEER_HEADS, PEER_NKEYS, tb), lambda i, j: (0, 0, i), pipeline_mode=once)
    return pl.pallas_call(
        functools.partial(_experts_body, nsub=nsub),
        grid=(S // tb, ne),
        in_specs=[
            pl.BlockSpec((tb, D), lambda i, j: (i, 0), pipeline_mode=once),
            pl.BlockSpec((te, D), lambda i, j: (j, 0)),
            pl.BlockSpec((te, D), lambda i, j: (j, 0)),
            grouped, grouped, whole, whole,
            pl.BlockSpec((PEER_HEADS, tb), lambda i, j: (0, i)),
            pl.BlockSpec((rr, D), lambda i, j: (i * ne + j, 0)),
        ],
        out_specs=pl.BlockSpec((tb, D), lambda i, j: (i, 0), pipeline_mode=once),
        out_shape=jax.ShapeDtypeStruct((S, D), F32),
        compiler_params=_params(60, "arbitrary", "arbitrary"),
        name="peer_experts",
    )(h2, u, v, s1, e1, s2, e2, tau, res)


def _relayout_w_tail(w_in, D):
    fh, W, off, total = _tail_layout(D)
    sizes = (fh, W, HEAD_DIM, HEAD_DIM, IDX_HEADS * IDX_DIM, IDX_DIM, IDX_HEADS)
    pts = (3 * W + np.cumsum(sizes)[:-1]).tolist()
    _, ff, dq, dk, dv, iq, ik, iw = jnp.split(w_in, [3 * W] + pts, axis=1)
    zeros = lambda n: jnp.zeros((D, n), w_in.dtype)
    misc = jnp.concatenate([ik, iw, zeros(LANES - IDX_DIM - IDX_HEADS)], axis=1)
    ffp = jnp.concatenate([ff, zeros(LANES - fh)], axis=1)
    cols = [dq, iq, dk, dv, misc, ffp]
    used = sum(c.shape[1] for c in cols)
    if total > used:
        cols.append(zeros(total - used))
    return jnp.concatenate(cols, axis=1).astype(BF16)


def _layer(x, positions, ln1_g, w_in, fox_forget_b, fox_qn_g, fox_kn_g, dsa_qn_g, dsa_kn_g,
           w_o, ln2_g, peer_wq, peer_keys1, peer_keys2, peer_u, peer_v):
    S, D = x.shape
    W = D // 2
    topk = min(DSA_TOPK_MAX, S // 4)

    h1 = _rmsnorm(x, ln1_g, BF16)
    proj_a = _matmul(h1, w_in, n_cols=3 * W, name="in_proj_fox")
    proj_b = _matmul(h1, _relayout_w_tail(w_in, D), name="in_proj_dsa")
    fox_safe, fox_half = _logit_span(fox_qn_g, fox_kn_g)
    dsa_safe, dsa_half = _logit_span(dsa_qn_g, dsa_kn_g)
    fq, fk, fv, dq, dk, dv, iq, ik, iw = _post(proj_a, proj_b, positions, fox_forget_b, fox_qn_g, fox_kn_g,
                                               dsa_qn_g, dsa_kn_g, fox_half, dsa_half, D)
    fox_out = lax.cond(fox_safe, functools.partial(_fox_attention, track_max=False),
                       functools.partial(_fox_attention, track_max=True), fq, fk, fv)
    dsa_out = lax.cond(dsa_safe, functools.partial(_dsa_attention, topk=topk, track_max=False),
                       functools.partial(_dsa_attention, topk=topk, track_max=True), iq, ik, iw, dq, dk, dv)
    x1 = _out_proj(fox_out, dsa_out, w_o, x)

    h2 = _rmsnorm(x1, ln2_g, BF16)
    qt = _matmul(peer_wq.T.astype(BF16), h2, nt=True, name="peer_query")
    zk = jnp.zeros_like(peer_keys1)
    kk = jnp.concatenate([jnp.concatenate([peer_keys1, zk], axis=2),
                          jnp.concatenate([zk, peer_keys2], axis=2)], axis=1).astype(F32)
    s1, e1, s2, e2, tau = _peer_route(qt, kk)
    return _peer_experts(h2, peer_u.astype(BF16), peer_v.astype(BF16), s1, e1, s2, e2, tau, x1)


def kernel(x, positions, ln1_g, w_in, fox_forget_b, fox_qn_g, fox_kn_g, dsa_qn_g, dsa_kn_g, w_o, ln2_g, peer_wq,
           peer_keys1, peer_keys2, peer_u, peer_v):
    B = x.shape[0]
    depth = w_in.shape[0]
    outs = []
    for b in range(B):
        xb = x[b]
        for l in range(depth):
            xb = _layer(xb, positions[b], ln1_g[l], w_in[l], fox_forget_b[l], fox_qn_g[l], fox_kn_g[l],
                        dsa_qn_g[l], dsa_kn_g[l], w_o[l], ln2_g[l], peer_wq[l], peer_keys1[l], peer_keys2[l],
                        peer_u[l], peer_v[l])
        outs.append(xb)
    return outs[0][None] if B == 1 else jnp.stack(outs, axis=0)
```

```python
import functools

import numpy as np
import jax
import jax.numpy as jnp
from jax import lax
from jax.experimental import pallas as pl
from jax.experimental.pallas import tpu as pltpu

F32 = jnp.float32
BF16 = jnp.bfloat16
I32 = jnp.int32

HEAD_DIM = 128
IDX_HEADS = 16
IDX_DIM = 64
DSA_TOPK_MAX = 256
ROPE_THETA = 10000.0
NORM_EPS = 1e-6
PEER_HEADS = 8
PEER_NKEYS = 128
PEER_SUBDIM = 64
PEER_TOPK = 16
LOG2E = 1.4426950408889634
LANES = 128
WIDE = 2 * HEAD_DIM
QSCALE = (HEAD_DIM ** -0.5) * LOG2E

LOGIT_BOUND_PER_GAIN = (HEAD_DIM ** 0.5) * LOG2E
SAFE_LOGIT_SPAN = 60.0

NEG_INF_KEY = -2139095041
INT32_MIN = -2147483648

NT_DIMS = (((1,), (1,)), ((), ()))
NN_DIMS = (((1,), (0,)), ((), ()))


def _params(vmem_mb, *sem):
    return pltpu.CompilerParams(dimension_semantics=sem, vmem_limit_bytes=vmem_mb * 1024 * 1024)


def _as_bf16(x):
    return x if x.dtype == BF16 else x.astype(BF16)


def _rmsnorm_body(x_ref, g_ref, o_ref):
    x = x_ref[...].astype(F32)
    y = x * lax.rsqrt(jnp.mean(x * x, axis=-1, keepdims=True) + NORM_EPS)
    o_ref[...] = (y * g_ref[...]).astype(o_ref.dtype)


def _rmsnorm(x2d, g, out_dtype, tm=256):
    S, D = x2d.shape
    tm = min(tm, S)
    return pl.pallas_call(
        _rmsnorm_body,
        grid=(S // tm,),
        in_specs=[pl.BlockSpec((tm, D), lambda i: (i, 0)), pl.BlockSpec((1, D), lambda i: (0, 0))],
        out_specs=pl.BlockSpec((tm, D), lambda i: (i, 0)),
        out_shape=jax.ShapeDtypeStruct((S, D), out_dtype),
        compiler_params=_params(40, "arbitrary"),
        name="rmsnorm",
    )(x2d, g.reshape(1, D).astype(F32))


def _matmul_body(a_ref, b_ref, o_ref, *, nt):
    acc = lax.dot_general(a_ref[...], _as_bf16(b_ref[...]), NT_DIMS if nt else NN_DIMS, preferred_element_type=F32)
    o_ref[...] = acc.astype(o_ref.dtype)


def _matmul(a, b, *, nt=False, n_cols=None, out_dtype=F32, tm=1024, tn=512, name="matmul"):
    M, K = a.shape
    N = n_cols if n_cols is not None else (b.shape[0] if nt else b.shape[1])
    tm = min(tm, M)
    tn = next(t for t in range(min(tn, N), 0, -LANES) if N % t == 0)
    assert M % tm == 0 and tn % LANES == 0
    return pl.pallas_call(
        functools.partial(_matmul_body, nt=nt),
        grid=(M // tm, N // tn),
        in_specs=[
            pl.BlockSpec((tm, K), lambda i, j: (i, 0)),
            pl.BlockSpec((tn, K), lambda i, j: (j, 0)) if nt else pl.BlockSpec((K, tn), lambda i, j: (0, j)),
        ],
        out_specs=pl.BlockSpec((tm, tn), lambda i, j: (i, j)),
        out_shape=jax.ShapeDtypeStruct((M, N), out_dtype),
        compiler_params=_params(48, "arbitrary", "arbitrary"),
        name=name,
    )(a, b)


def _out_proj_body(a1_ref, a2_ref, b1_ref, b2_ref, r_ref, o_ref):
    acc = jnp.dot(a1_ref[...], _as_bf16(b1_ref[...]), preferred_element_type=F32)
    acc = acc + jnp.dot(a2_ref[...], _as_bf16(b2_ref[...]), preferred_element_type=F32)
    o_ref[...] = acc + r_ref[...]


def _out_proj(a1, a2, w, res, tm=1024, tn=512):
    M, K1 = a1.shape
    N = w.shape[1]
    assert a2.shape == a1.shape and w.shape[0] == 2 * K1
    tm, tn = min(tm, M), min(tn, N)
    lhs = pl.BlockSpec((tm, K1), lambda i, j: (i, 0))
    return pl.pallas_call(
        _out_proj_body,
        grid=(M // tm, N // tn),
        in_specs=[lhs, lhs, pl.BlockSpec((K1, tn), lambda i, j: (0, j)), pl.BlockSpec((K1, tn), lambda i, j: (1, j)),
                  pl.BlockSpec((tm, tn), lambda i, j: (i, j))],
        out_specs=pl.BlockSpec((tm, tn), lambda i, j: (i, j)),
        out_shape=jax.ShapeDtypeStruct((M, N), F32),
        compiler_params=_params(48, "arbitrary", "arbitrary"),
        name="out_proj",
    )(a1, a2, w, w, res)


def _tail_layout(D):
    fh = D // (2 * HEAD_DIM)
    W = fh * HEAD_DIM
    off = {}
    pos = 0
    for name, width in (("dq", W), ("iq", IDX_HEADS * IDX_DIM), ("dk", HEAD_DIM), ("dv", HEAD_DIM),
                        ("misc", LANES), ("ff", LANES)):
        off[name] = pos
        pos += width
    total = -(-pos // 512) * 512
    return fh, W, off, total


def _post_body(pa_ref, pb_ref, pos_ref, fb_ref, fqg_ref, fkg_ref, dqg_ref, dkg_ref, inv1_ref, sgn1_ref, inv2_ref,
               sgn2_ref, hbf_ref, hbd_ref,
               fq_ref, fk_ref, fv_ref, dq_ref, dk_ref, dv_ref, iq_ref, ik_ref, iw_ref, carry_ref,
               *, fh, off, tm):
    @pl.when(pl.program_id(0) == 0)
    def _():
        carry_ref[...] = jnp.zeros_like(carry_ref)

    W = fh * HEAD_DIM
    pos = pos_ref[...].astype(F32)
    ang1 = pos * inv1_ref[...]
    cos1, sin1 = jnp.cos(ang1), jnp.sin(ang1) * sgn1_ref[...]
    ang2 = pos * inv2_ref[...]
    cos2, sin2 = jnp.cos(ang2), jnp.sin(ang2) * sgn2_ref[...]
    lane = lax.broadcasted_iota(I32, (tm, LANES), 1)
    lo_half = (lane & (IDX_DIM - 1)) < (IDX_DIM // 2)
    one_hot0 = jnp.where(lane == 0, 1.0, 0.0)

    def head(seg, h):
        s = seg * W + h * LANES
        return pa_ref[:, s:s + LANES]

    def tail(name, j=0):
        s = off[name] + j * LANES
        return pb_ref[:, s:s + LANES]

    def norm(xh, g_ref):
        return xh * lax.rsqrt(jnp.mean(xh * xh, axis=-1, keepdims=True) + NORM_EPS) * g_ref[...]

    def rope_full(xh):
        return xh * cos1 + pltpu.roll(xh, HEAD_DIM // 2, 1) * sin1

    def rope_idx(xh):
        r = jnp.where(lo_half, pltpu.roll(xh, LANES - IDX_DIM // 2, 1), pltpu.roll(xh, IDX_DIM // 2, 1))
        return xh * cos2 + r * sin2

    z = tail("ff") + fb_ref[...]
    c = jnp.minimum(z, 0.0) - jnp.log1p(jnp.exp(-jnp.abs(z)))
    row = lax.broadcasted_iota(I32, (tm, LANES), 0)
    s = 1
    while s < tm:
        c = c + jnp.where(row >= s, pltpu.roll(c, s, 0), 0.0)
        s *= 2
    c = c + carry_ref[...]
    carry_ref[...] = c[tm - 1:tm, :]
    nc = (-LOG2E) * c

    for h in range(fh):
        hs = slice(h * WIDE, h * WIDE + HEAD_DIM)
        xs = slice(h * WIDE + HEAD_DIM, (h + 1) * WIDE)
        fq_ref[:, hs] = (norm(head(0, h), fqg_ref) * QSCALE).astype(BF16)
        fk_ref[:, hs] = norm(head(1, h), fkg_ref).astype(BF16)
        fv_ref[:, hs] = head(2, h).astype(BF16)
        b = jnp.broadcast_to(nc[:, h:h + 1], (tm, LANES))
        hi = b.astype(BF16).astype(F32)
        mid = (b - hi).astype(BF16).astype(F32)
        lo = b - hi - mid
        k_aug = jnp.where(lane == 0, hi, jnp.where(lane == 1, mid, jnp.where(lane == 2, lo,
                                                                             jnp.where(lane == 3, 1.0, 0.0))))
        q_aug = jnp.where(lane < 3, 1.0, jnp.where(lane == 3, -b - hbf_ref[...], 0.0))
        fk_ref[:, xs] = k_aug.astype(BF16)
        fq_ref[:, xs] = q_aug.astype(BF16)
        fv_ref[:, xs] = one_hot0.astype(BF16)
        dq_ref[:, h * LANES:(h + 1) * LANES] = (rope_full(norm(tail("dq", h), dqg_ref)) * QSCALE).astype(BF16)
    dk_ref[:, :HEAD_DIM] = rope_full(norm(tail("dk"), dkg_ref)).astype(BF16)
    dk_ref[:, HEAD_DIM:] = (one_hot0 * (-hbd_ref[...])).astype(BF16)
    dv_ref[:, :HEAD_DIM] = tail("dv").astype(BF16)
    dv_ref[:, HEAD_DIM:] = one_hot0.astype(BF16)

    for j in range(IDX_HEADS // 2):
        blk = rope_idx(tail("iq", j)) * (IDX_DIM ** -0.5)
        iq_ref[2 * j] = blk[:, :IDX_DIM].astype(BF16)
        iq_ref[2 * j + 1] = blk[:, IDX_DIM:].astype(BF16)
    misc = tail("misc")
    ik_ref[...] = rope_idx(misc)[:, :IDX_DIM].astype(BF16)
    iw_ref[...] = misc[:, IDX_DIM:IDX_DIM + IDX_HEADS] * (IDX_HEADS ** -0.5)


def _post(proj_a, proj_b, positions, fb, fqg, fkg, dqg, dkg, half_span_f, half_span_d, D, tm=128):
    S, NB = proj_b.shape
    fh, W, off, total = _tail_layout(D)
    assert total == NB and proj_a.shape == (S, 3 * W)
    tm = min(tm, S)

    def pad_lanes(v):
        return jnp.zeros((1, LANES), F32).at[0, :v.shape[0]].set(v.astype(F32))

    half = HEAD_DIM // 2
    inv_full = ROPE_THETA ** (-jnp.arange(0, HEAD_DIM, 2, dtype=F32) / HEAD_DIM)
    inv1 = jnp.concatenate([inv_full, inv_full]).reshape(1, LANES)
    sgn1 = jnp.concatenate([-jnp.ones(half, F32), jnp.ones(half, F32)]).reshape(1, LANES)
    inv_idx = ROPE_THETA ** (-jnp.arange(0, IDX_DIM, 2, dtype=F32) / IDX_DIM)
    inv2 = jnp.tile(inv_idx, LANES // (IDX_DIM // 2)).reshape(1, LANES)
    q = IDX_DIM // 2
    sgn2 = jnp.tile(jnp.concatenate([-jnp.ones(q, F32), jnp.ones(q, F32)]), LANES // IDX_DIM).reshape(1, LANES)

    row = lambda w: pl.BlockSpec((tm, w), lambda i: (i, 0))
    const = pl.BlockSpec((1, LANES), lambda i: (0, 0))
    wide = jax.ShapeDtypeStruct((S, fh * WIDE), BF16)
    out_shape = (
        wide, wide, wide, jax.ShapeDtypeStruct((S, W), BF16), jax.ShapeDtypeStruct((S, WIDE), BF16),
        jax.ShapeDtypeStruct((S, WIDE), BF16), jax.ShapeDtypeStruct((IDX_HEADS, S, IDX_DIM), BF16),
        jax.ShapeDtypeStruct((S, IDX_DIM), BF16), jax.ShapeDtypeStruct((S, IDX_HEADS), F32),
    )
    out_specs = (
        row(fh * WIDE), row(fh * WIDE), row(fh * WIDE), row(W), row(WIDE), row(WIDE),
        pl.BlockSpec((IDX_HEADS, tm, IDX_DIM), lambda i: (0, i, 0)), row(IDX_DIM), row(IDX_HEADS),
    )
    return pl.pallas_call(
        functools.partial(_post_body, fh=fh, off=off, tm=tm),
        grid=(S // tm,),
        in_specs=[row(3 * W), row(NB), row(1)] + [const] * 11,
        out_specs=out_specs,
        out_shape=out_shape,
        scratch_shapes=[pltpu.VMEM((1, LANES), F32)],
        compiler_params=_params(40, "arbitrary"),
        name="post",
    )(proj_a, proj_b, positions.reshape(S, 1).astype(I32), pad_lanes(fb), fqg.reshape(1, LANES).astype(F32),
      fkg.reshape(1, LANES).astype(F32), dqg.reshape(1, LANES).astype(F32), dkg.reshape(1, LANES).astype(F32),
      inv1, sgn1, inv2, sgn2, jnp.full((1, LANES), half_span_f, F32), jnp.full((1, LANES), half_span_d, F32))


def _attend(s, v, acc_ref, m_ref, track_max):
    if not track_max:
        acc_ref[...] += jnp.dot(jnp.exp2(s).astype(BF16), v, preferred_element_type=F32)
        return
    m_prev = m_ref[...]
    m_new = jnp.maximum(m_prev, jnp.max(s, axis=-1, keepdims=True))
    m_safe = jnp.where(m_new == -jnp.inf, 0.0, m_new)
    alpha = jnp.exp2(m_prev - m_safe)
    p = jnp.exp2(s - jnp.tile(m_safe, (1, s.shape[1] // LANES)))
    acc_ref[...] = jnp.tile(alpha, (1, WIDE // LANES)) * acc_ref[...] + jnp.dot(p.astype(BF16), v,
                                                                                 preferred_element_type=F32)
    m_ref[...] = m_new


def _normalised(acc):
    return acc[:, :HEAD_DIM] / acc[:, HEAD_DIM:HEAD_DIM + 1]


def _logit_span(gq, gk):
    half = LOGIT_BOUND_PER_GAIN * jnp.max(jnp.abs(gq.astype(F32))) * jnp.max(jnp.abs(gk.astype(F32)))
    return 2.0 * half <= SAFE_LOGIT_SPAN, half


def _fox_body(q_ref, k_ref, v_ref, o_ref, acc_ref, m_ref, *, tb, hp, track_max):
    qi = pl.program_id(1)
    acc_ref[...] = jnp.zeros_like(acc_ref)
    if track_max:
        m_ref[...] = jnp.full_like(m_ref, -jnp.inf)

    def step(ki, masked):
        ks = pl.multiple_of(ki * tb, tb)
        for h in range(hp):
            ws = slice(h * WIDE, (h + 1) * WIDE)
            s = lax.dot_general(q_ref[:, ws], k_ref[pl.ds(ks, tb), ws], NT_DIMS, preferred_element_type=F32)
            if masked:
                r = lax.broadcasted_iota(I32, (tb, tb), 0)
                c = lax.broadcasted_iota(I32, (tb, tb), 1)
                s = jnp.where(c <= r, s, -jnp.inf)
            _attend(s, v_ref[pl.ds(ks, tb), ws], acc_ref.at[h], m_ref.at[h], track_max)

    def loop_body(ki, carry):
        step(ki, False)
        return carry

    lax.fori_loop(0, qi, loop_body, 0)
    step(qi, True)
    for h in range(hp):
        o_ref[:, h * HEAD_DIM:(h + 1) * HEAD_DIM] = _normalised(acc_ref[h]).astype(o_ref.dtype)


def _fox_attention(fq, fk, fv, *, track_max, tb=512, hp=4):
    S = fq.shape[0]
    fh = fq.shape[1] // WIDE
    tb = min(tb, S)
    hp = min(hp, fh)
    once = pl.Buffered(1)
    return pl.pallas_call(
        functools.partial(_fox_body, tb=tb, hp=hp, track_max=track_max),
        grid=(fh // hp, S // tb),
        in_specs=[
            pl.BlockSpec((tb, hp * WIDE), lambda h, i: (i, h)),
            pl.BlockSpec((S, hp * WIDE), lambda h, i: (0, h), pipeline_mode=once),
            pl.BlockSpec((S, hp * WIDE), lambda h, i: (0, h), pipeline_mode=once),
        ],
        out_specs=pl.BlockSpec((tb, hp * HEAD_DIM), lambda h, i: (i, h)),
        out_shape=jax.ShapeDtypeStruct((S, fh * HEAD_DIM), BF16),
        scratch_shapes=[pltpu.VMEM((hp, tb, WIDE), F32), pltpu.VMEM((hp, tb, LANES), F32)],
        compiler_params=_params(56, "arbitrary", "arbitrary"),
        name="fox_attention",
    )(fq, fk, fv)


def _sortable_key(x):
    b = lax.bitcast_convert_type(x, I32)
    return b ^ ((b >> 31) & 0x7FFFFFFF)


def _dsa_body(iq_ref, ik_ref, iw_ref, dq_ref, dk_ref, dv_ref, o_ref,
              keys_ref, w_ref, q_ref, acc_ref, m_ref, *, tq, tk, topk, nh, track_max):
    qi = pl.program_id(0)
    groups = max([1] + [g for g in (2, 4, 8) if nh % g == 0 and IDX_HEADS % g == 0 and g * GROUP_ROWS <= nh * tq])
    ig, ag = IDX_HEADS // groups, nh // groups
    nkb = (qi * tq) // tk + 1
    row_g = qi * tq + lax.broadcasted_iota(I32, (tq, tk), 0)
    col_l = lax.broadcasted_iota(I32, (tq, tk), 1)

    for h in range(IDX_HEADS):
        w_ref[h * tq:(h + 1) * tq, :] = iw_ref[:, h:h + 1]

    def score_chunk(kb, carry):
        ks = pl.multiple_of(kb * tk, tk)
        ikb = ik_ref[pl.ds(ks, tk), :]
        sc = None
        for g in range(groups):
            rows = slice(g * ig * tq, (g + 1) * ig * tq)
            iq_g = iq_ref[g * ig:(g + 1) * ig].reshape(ig * tq, IDX_DIM)
            logits = lax.dot_general(iq_g, ikb, NT_DIMS, preferred_element_type=F32)
            part = jnp.sum((jnp.maximum(logits, 0.0) * w_ref[rows, :]).reshape(ig, tq, tk), axis=0)
            sc = part if sc is None else sc + part
        sc = jnp.where(ks + col_l <= row_g, sc, -jnp.inf)
        keys_ref[kb] = _sortable_key(sc)
        return carry

    lax.fori_loop(0, nkb, score_chunk, 0)

    def bit_step(i, thr):
        cand = thr + lax.shift_left(jnp.int32(1), 31 - i)

        cand_rows = [jnp.broadcast_to(cand[r:r + COUNT_ROWS], (COUNT_ROWS, LANES)) for r in range(0, tq, COUNT_ROWS)]

        def count_chunk(kb, cnts):
            out = []
            for n, r in enumerate(range(0, tq, COUNT_ROWS)):
                c = cnts[n]
                for j in range(tk // LANES):
                    k = keys_ref[kb, r:r + COUNT_ROWS, j * LANES:(j + 1) * LANES]
                    c = c + jnp.where(k >= cand_rows[n], 1.0, 0.0)
                out.append(c)
            return tuple(out)

        zero = jnp.zeros((COUNT_ROWS, LANES), F32)
        cnts = lax.fori_loop(0, nkb, count_chunk, (zero,) * (tq // COUNT_ROWS))
        cnt = jnp.sum(jnp.concatenate(cnts, axis=0), axis=-1, keepdims=True)
        return jnp.where(cnt >= float(topk), cand, thr)

    thr = lax.fori_loop(0, 32, bit_step, jnp.full((tq, 1), INT32_MIN, I32))
    thr = jnp.maximum(thr, NEG_INF_KEY + 1)

    lane = lax.broadcasted_iota(I32, (nh * tq, LANES), 1)
    q_ref[:, HEAD_DIM:] = jnp.where(lane == 0, 1.0, 0.0).astype(BF16)
    for h in range(nh):
        q_ref[h * tq:(h + 1) * tq, :HEAD_DIM] = dq_ref[:, h * HEAD_DIM:(h + 1) * HEAD_DIM]
    acc_ref[...] = jnp.zeros_like(acc_ref)
    if track_max:
        m_ref[...] = jnp.full_like(m_ref, -jnp.inf)

    def attn_chunk(kb, carry):
        ks = pl.multiple_of(kb * tk, tk)
        k = dk_ref[pl.ds(ks, tk), :]
        v = dv_ref[pl.ds(ks, tk), :]
        drop = jnp.where(keys_ref[kb] >= thr, 0.0, -jnp.inf)
        for g in range(groups):
            rows = pl.ds(g * ag * tq, ag * tq)
            s = lax.dot_general(q_ref[rows, :], k, NT_DIMS, preferred_element_type=F32)
            s = (s.reshape(ag, tq, tk) + drop[None]).reshape(ag * tq, tk)
            _attend(s, v, acc_ref.at[rows], m_ref.at[rows], track_max)
        return carry

    lax.fori_loop(0, nkb, attn_chunk, 0)
    out = _normalised(acc_ref[...])
    for h in range(nh):
        o_ref[:, h * HEAD_DIM:(h + 1) * HEAD_DIM] = out[h * tq:(h + 1) * tq, :].astype(o_ref.dtype)


GROUP_ROWS = 1024
COUNT_ROWS = 64


def _dsa_attention(iq, ik, iw, dq, dk, dv, *, topk, track_max, tq=256, tk=512):
    S, W = dq.shape
    nh = W // HEAD_DIM
    tq, tk = min(tq, S), min(tk, S)
    nkb = S // tk
    full = lambda shape: pl.BlockSpec(shape, lambda i: (0,) * len(shape))
    return pl.pallas_call(
        functools.partial(_dsa_body, tq=tq, tk=tk, topk=topk, nh=nh, track_max=track_max),
        grid=(S // tq,),
        in_specs=[
            pl.BlockSpec((IDX_HEADS, tq, IDX_DIM), lambda i: (0, i, 0)),
            full((S, IDX_DIM)),
            pl.BlockSpec((tq, IDX_HEADS), lambda i: (i, 0)),
            pl.BlockSpec((tq, W), lambda i: (i, 0)),
            full((S, WIDE)),
            full((S, WIDE)),
        ],
        out_specs=pl.BlockSpec((tq, W), lambda i: (i, 0)),
        out_shape=jax.ShapeDtypeStruct((S, W), BF16),
        scratch_shapes=[
            pltpu.VMEM((nkb, tq, tk), I32),
            pltpu.VMEM((IDX_HEADS * tq, 1), F32),
            pltpu.VMEM((nh * tq, WIDE), BF16),
            pltpu.VMEM((nh * tq, WIDE), F32),
            pltpu.VMEM((nh * tq, LANES), F32),
        ],
        compiler_params=_params(58, "arbitrary"),
        name="dsa_attention",
    )(iq, ik, iw, dq, dk, dv)


ROUTE_GROUP = 8


def _top_rows(s, n):
    rows = []
    for _ in range(n):
        m = jnp.max(s, axis=0, keepdims=True)
        rows.append(m)
        s = jnp.where(s == m, -jnp.inf, s)
    return rows


def _route_body(qt_ref, kk_ref, s1_ref, e1_ref, s2_ref, e2_ref, tau_ref):
    ts = qt_ref.shape[1]
    ng = PEER_NKEYS // ROUTE_GROUP
    for h in range(PEER_HEADS):
        qh = qt_ref[h * LANES:(h + 1) * LANES, :]
        sc = jnp.dot(kk_ref[h], qh, preferred_element_type=F32, precision=lax.Precision.HIGHEST)
        s1, s2 = sc[:PEER_NKEYS], sc[PEER_NKEYS:]
        v1 = _top_rows(s1, PEER_TOPK)
        v2 = jnp.concatenate(_top_rows(s2, PEER_TOPK), axis=0)
        cand = [v1[a] + v2[:-(-(PEER_TOPK // (a + 1)) // 8) * 8] for a in range(PEER_TOPK)]
        best = _top_rows(jnp.concatenate(cand, axis=0), PEER_TOPK)
        z = sum(jnp.exp(b - best[0]) for b in best)
        s1_ref[:, h] = s1.reshape(ng, ROUTE_GROUP, ts)
        e1_ref[:, h] = (jnp.exp(s1 - v1[0]) / z).reshape(ng, ROUTE_GROUP, ts)
        s2_ref[h] = s2
        e2_ref[h] = jnp.exp(s2 - v2[0:1])
        tau_ref[h:h + 1, :] = best[PEER_TOPK - 1]


def _peer_route(qt, kk, ts=256):
    NQ, S = qt.shape
    ts = min(ts, S)
    ng = PEER_NKEYS // ROUTE_GROUP
    grouped = jax.ShapeDtypeStruct((ng, PEER_HEADS, ROUTE_GROUP, S), F32)
    whole = jax.ShapeDtypeStruct((PEER_HEADS, PEER_NKEYS, S), F32)
    gspec = pl.BlockSpec((ng, PEER_HEADS, ROUTE_GROUP, ts), lambda i: (0, 0, 0, i))
    wspec = pl.BlockSpec((PEER_HEADS, PEER_NKEYS, ts), lambda i: (0, 0, i))
    return pl.pallas_call(
        _route_body,
        grid=(S // ts,),
        in_specs=[pl.BlockSpec((NQ, ts), lambda i: (0, i)),
                  pl.BlockSpec((PEER_HEADS, 2 * PEER_NKEYS, LANES), lambda i: (0, 0, 0))],
        out_specs=(gspec, gspec, wspec, wspec, pl.BlockSpec((PEER_HEADS, ts), lambda i: (0, i))),
        out_shape=(grouped, grouped, whole, whole, jax.ShapeDtypeStruct((PEER_HEADS, S), F32)),
        compiler_params=_params(40, "arbitrary"),
        name="peer_route",
    )(qt, kk)


def _experts_body(x_ref, u_ref, v_ref, s1_ref, e1_ref, s2_ref, e2_ref, tau_ref, res_ref, o_ref, *, nsub):
    eb = pl.program_id(1)

    @pl.when(eb == 0)
    def _():
        o_ref[...] = jnp.zeros_like(o_ref)

    rr = res_ref.shape[0]
    r0 = pl.multiple_of(eb * rr, rr)
    o_ref[pl.ds(r0, rr), :] += res_ref[...]

    first = (eb * nsub) % ROUTE_GROUP
    tb = x_ref.shape[0]
    ht = lax.dot_general(u_ref[...], x_ref[...], NT_DIMS, preferred_element_type=F32)
    parts = []
    for j in range(nsub):
        gate = jnp.zeros((PEER_NKEYS, tb), F32)
        for h in range(PEER_HEADS):
            s1 = s1_ref[h, pl.ds(first + j, 1), :]
            e1 = e1_ref[h, pl.ds(first + j, 1), :]
            routed = (s1 + s2_ref[h]) >= tau_ref[h:h + 1, :]
            gate = gate + jnp.where(routed, e2_ref[h] * e1, 0.0)
        hj = ht[j * PEER_NKEYS:(j + 1) * PEER_NKEYS, :]
        act = 0.5 * hj * (1.0 + lax.erf(hj * (2.0 ** -0.5)))
        parts.append(gate * act)
    a = jnp.concatenate(parts, axis=0).T.astype(BF16)
    o_ref[...] += jnp.dot(a, v_ref[...], preferred_element_type=F32)


def _peer_experts(h2, u, v, s1, e1, s2, e2, tau, res, tb=512, te=1024):
    S, D = h2.shape
    E = u.shape[0]
    tb = min(tb, S)
    nsub = te // PEER_NKEYS
    ne = E // te
    rr = tb // ne
    assert ROUTE_GROUP % nsub == 0 and tb % ne == 0 and rr % 8 == 0
    grouped = pl.BlockSpec((None, PEER_HEADS, ROUTE_GROUP, tb), lambda i, j: ((j * nsub) // ROUTE_GROUP, 0, 0, i))
    once = pl.Buffered(1)
    whole = pl.BlockSpec((PEER_HEADS, PEER_NKEYS, tb), lambda i, j: (0, 0, i), pipeline_mode=once)
    return pl.pallas_call(
        functools.partial(_experts_body, nsub=nsub),
        grid=(S // tb, ne),
        in_specs=[
            pl.BlockSpec((tb, D), lambda i, j: (i, 0), pipeline_mode=once),
            pl.BlockSpec((te, D), lambda i, j: (j, 0)),
            pl.BlockSpec((te, D), lambda i, j: (j, 0)),
            grouped, grouped, whole, whole,
            pl.BlockSpec((PEER_HEADS, tb), lambda i, j: (0, i)),
            pl.BlockSpec((rr, D), lambda i, j: (i * ne + j, 0)),
        ],
        out_specs=pl.BlockSpec((tb, D), lambda i, j: (i, 0), pipeline_mode=once),
        out_shape=jax.ShapeDtypeStruct((S, D), F32),
        compiler_params=_params(60, "arbitrary", "arbitrary"),
        name="peer_experts",
    )(h2, u, v, s1, e1, s2, e2, tau, res)


def _relayout_w_tail(w_in, D):
    fh, W, off, total = _tail_layout(D)
    sizes = (fh, W, HEAD_DIM, HEAD_DIM, IDX_HEADS * IDX_DIM, IDX_DIM, IDX_HEADS)
    pts = (3 * W + np.cumsum(sizes)[:-1]).tolist()
    _, ff, dq, dk, dv, iq, ik, iw = jnp.split(w_in, [3 * W] + pts, axis=1)
    zeros = lambda n: jnp.zeros((D, n), w_in.dtype)
    misc = jnp.concatenate([ik, iw, zeros(LANES - IDX_DIM - IDX_HEADS)], axis=1)
    ffp = jnp.concatenate([ff, zeros(LANES - fh)], axis=1)
    cols = [dq, iq, dk, dv, misc, ffp]
    used = sum(c.shape[1] for c in cols)
    if total > used:
        cols.append(zeros(total - used))
    return jnp.concatenate(cols, axis=1).astype(BF16)


def _layer(x, positions, ln1_g, w_in, fox_forget_b, fox_qn_g, fox_kn_g, dsa_qn_g, dsa_kn_g,
           w_o, ln2_g, peer_wq, peer_keys1, peer_keys2, peer_u, peer_v):
    S, D = x.shape
    W = D // 2
    topk = min(DSA_TOPK_MAX, S // 4)

    h1 = _rmsnorm(x, ln1_g, BF16)
    proj_a = _matmul(h1, w_in[:, :3 * W].astype(BF16), name="in_proj_fox")
    proj_b = _matmul(h1, _relayout_w_tail(w_in, D), name="in_proj_dsa")
    fox_safe, fox_half = _logit_span(fox_qn_g, fox_kn_g)
    dsa_safe, dsa_half = _logit_span(dsa_qn_g, dsa_kn_g)
    fq, fk, fv, dq, dk, dv, iq, ik, iw = _post(proj_a, proj_b, positions, fox_forget_b, fox_qn_g, fox_kn_g,
                                               dsa_qn_g, dsa_kn_g, fox_half, dsa_half, D)
    fox_out = lax.cond(fox_safe, functools.partial(_fox_attention, track_max=False),
                       functools.partial(_fox_attention, track_max=True), fq, fk, fv)
    dsa_out = lax.cond(dsa_safe, functools.partial(_dsa_attention, topk=topk, track_max=False),
                       functools.partial(_dsa_attention, topk=topk, track_max=True), iq, ik, iw, dq, dk, dv)
    x1 = _out_proj(fox_out, dsa_out, w_o, x)

    h2 = _rmsnorm(x1, ln2_g, BF16)
    qt = _matmul(peer_wq.T.astype(BF16), h2, nt=True, name="peer_query")
    zk = jnp.zeros_like(peer_keys1)
    kk = jnp.concatenate([jnp.concatenate([peer_keys1, zk], axis=2),
                          jnp.concatenate([zk, peer_keys2], axis=2)], axis=1).astype(F32)
    s1, e1, s2, e2, tau = _peer_route(qt, kk)
    return _peer_experts(h2, peer_u.astype(BF16), peer_v.astype(BF16), s1, e1, s2, e2, tau, x1)


def kernel(x, positions, ln1_g, w_in, fox_forget_b, fox_qn_g, fox_kn_g, dsa_qn_g, dsa_kn_g, w_o, ln2_g, peer_wq,
           peer_keys1, peer_keys2, peer_u, peer_v):
    B = x.shape[0]
    depth = w_in.shape[0]
    outs = []
    for b in range(B):
        xb = x[b]
        for l in range(depth):
            xb = _layer(xb, positions[b], ln1_g[l], w_in[l], fox_forget_b[l], fox_qn_g[l], fox_kn_g[l],
                        dsa_qn_g[l], dsa_kn_g[l], w_o[l], ln2_g[l], peer_wq[l], peer_keys1[l], peer_keys2[l],
                        peer_u[l], peer_v[l])
        outs.append(xb)
    return outs[0][None] if B == 1 else jnp.stack(outs, axis=0)
```

```python
import functools

import numpy as np
import jax
import jax.numpy as jnp
from jax import lax
from jax.experimental import pallas as pl
from jax.experimental.pallas import tpu as pltpu

F32 = jnp.float32
BF16 = jnp.bfloat16
I32 = jnp.int32

HEAD_DIM = 128
IDX_HEADS = 16
IDX_DIM = 64
DSA_TOPK_MAX = 256
ROPE_THETA = 10000.0
NORM_EPS = 1e-6
PEER_HEADS = 8
PEER_NKEYS = 128
PEER_SUBDIM = 64
PEER_TOPK = 16
LOG2E = 1.4426950408889634
LANES = 128
WIDE = 2 * HEAD_DIM
QSCALE = (HEAD_DIM ** -0.5) * LOG2E

LOGIT_BOUND_PER_GAIN = (HEAD_DIM ** 0.5) * LOG2E
SAFE_LOGIT_SPAN = 60.0

NEG_INF_KEY = -2139095041
INT32_MIN = -2147483648

NT_DIMS = (((1,), (1,)), ((), ()))
NN_DIMS = (((1,), (0,)), ((), ()))


def _params(vmem_mb, *sem):
    return pltpu.CompilerParams(dimension_semantics=sem, vmem_limit_bytes=vmem_mb * 1024 * 1024)


def _as_bf16(x):
    return x if x.dtype == BF16 else x.astype(BF16)


def _rmsnorm_body(x_ref, g_ref, o_ref):
    x = x_ref[...].astype(F32)
    y = x * lax.rsqrt(jnp.mean(x * x, axis=-1, keepdims=True) + NORM_EPS)
    o_ref[...] = (y * g_ref[...]).astype(o_ref.dtype)


def _rmsnorm(x2d, g, out_dtype, tm=256):
    S, D = x2d.shape
    tm = min(tm, S)
    return pl.pallas_call(
        _rmsnorm_body,
        grid=(S // tm,),
        in_specs=[pl.BlockSpec((tm, D), lambda i: (i, 0)), pl.BlockSpec((1, D), lambda i: (0, 0))],
        out_specs=pl.BlockSpec((tm, D), lambda i: (i, 0)),
        out_shape=jax.ShapeDtypeStruct((S, D), out_dtype),
        compiler_params=_params(40, "arbitrary"),
        name="rmsnorm",
    )(x2d, g.reshape(1, D).astype(F32))


def _matmul_body(a_ref, b_ref, o_ref, *, nt):
    acc = lax.dot_general(a_ref[...], _as_bf16(b_ref[...]), NT_DIMS if nt else NN_DIMS, preferred_element_type=F32)
    o_ref[...] = acc.astype(o_ref.dtype)


def _matmul(a, b, *, nt=False, n_cols=None, out_dtype=F32, tm=1024, tn=512, name="matmul"):
    M, K = a.shape
    N = n_cols if n_cols is not None else (b.shape[0] if nt else b.shape[1])
    tm = min(tm, M)
    tn = next(t for t in range(min(tn, N), 0, -LANES) if N % t == 0)
    assert M % tm == 0 and tn % LANES == 0
    return pl.pallas_call(
        functools.partial(_matmul_body, nt=nt),
        grid=(M // tm, N // tn),
        in_specs=[
            pl.BlockSpec((tm, K), lambda i, j: (i, 0)),
            pl.BlockSpec((tn, K), lambda i, j: (j, 0)) if nt else pl.BlockSpec((K, tn), lambda i, j: (0, j)),
        ],
        out_specs=pl.BlockSpec((tm, tn), lambda i, j: (i, j)),
        out_shape=jax.ShapeDtypeStruct((M, N), out_dtype),
        compiler_params=_params(48, "arbitrary", "arbitrary"),
        name=name,
    )(a, b)


def _out_proj_body(a1_ref, a2_ref, b1_ref, b2_ref, r_ref, o_ref):
    acc = jnp.dot(a1_ref[...], _as_bf16(b1_ref[...]), preferred_element_type=F32)
    acc = acc + jnp.dot(a2_ref[...], _as_bf16(b2_ref[...]), preferred_element_type=F32)
    o_ref[...] = acc + r_ref[...]


def _out_proj(a1, a2, w, res, tm=1024, tn=512):
    M, K1 = a1.shape
    N = w.shape[1]
    assert a2.shape == a1.shape and w.shape[0] == 2 * K1
    tm, tn = min(tm, M), min(tn, N)
    lhs = pl.BlockSpec((tm, K1), lambda i, j: (i, 0))
    return pl.pallas_call(
        _out_proj_body,
        grid=(M // tm, N // tn),
        in_specs=[lhs, lhs, pl.BlockSpec((K1, tn), lambda i, j: (0, j)), pl.BlockSpec((K1, tn), lambda i, j: (1, j)),
                  pl.BlockSpec((tm, tn), lambda i, j: (i, j))],
        out_specs=pl.BlockSpec((tm, tn), lambda i, j: (i, j)),
        out_shape=jax.ShapeDtypeStruct((M, N), F32),
        compiler_params=_params(48, "arbitrary", "arbitrary"),
        name="out_proj",
    )(a1, a2, w, w, res)


def _tail_layout(D):
    fh = D // (2 * HEAD_DIM)
    W = fh * HEAD_DIM
    off = {}
    pos = 0
    for name, width in (("dq", W), ("iq", IDX_HEADS * IDX_DIM), ("dk", HEAD_DIM), ("dv", HEAD_DIM),
                        ("misc", LANES), ("ff", LANES)):
        off[name] = pos
        pos += width
    total = -(-pos // 512) * 512
    return fh, W, off, total


def _post_body(pa_ref, pb_ref, pos_ref, fb_ref, fqg_ref, fkg_ref, dqg_ref, dkg_ref, inv1_ref, sgn1_ref, inv2_ref,
               sgn2_ref, hbf_ref, hbd_ref,
               fq_ref, fk_ref, fv_ref, dq_ref, dk_ref, dv_ref, iq_ref, ik_ref, iw_ref, carry_ref,
               *, fh, off, tm):
    @pl.when(pl.program_id(0) == 0)
    def _():
        carry_ref[...] = jnp.zeros_like(carry_ref)

    W = fh * HEAD_DIM
    pos = pos_ref[...].astype(F32)
    ang1 = pos * inv1_ref[...]
    cos1, sin1 = jnp.cos(ang1), jnp.sin(ang1) * sgn1_ref[...]
    ang2 = pos * inv2_ref[...]
    cos2, sin2 = jnp.cos(ang2), jnp.sin(ang2) * sgn2_ref[...]
    lane = lax.broadcasted_iota(I32, (tm, LANES), 1)
    lo_half = (lane & (IDX_DIM - 1)) < (IDX_DIM // 2)
    one_hot0 = jnp.where(lane == 0, 1.0, 0.0)

    def head(seg, h):
        s = seg * W + h * LANES
        return pa_ref[:, s:s + LANES]

    def tail(name, j=0):
        s = off[name] + j * LANES
        return pb_ref[:, s:s + LANES]

    def norm(xh, g_ref):
        return xh * lax.rsqrt(jnp.mean(xh * xh, axis=-1, keepdims=True) + NORM_EPS) * g_ref[...]

    def rope_full(xh):
        return xh * cos1 + pltpu.roll(xh, HEAD_DIM // 2, 1) * sin1

    def rope_idx(xh):
        r = jnp.where(lo_half, pltpu.roll(xh, LANES - IDX_DIM // 2, 1), pltpu.roll(xh, IDX_DIM // 2, 1))
        return xh * cos2 + r * sin2

    z = tail("ff") + fb_ref[...]
    c = jnp.minimum(z, 0.0) - jnp.log1p(jnp.exp(-jnp.abs(z)))
    row = lax.broadcasted_iota(I32, (tm, LANES), 0)
    s = 1
    while s < tm:
        c = c + jnp.where(row >= s, pltpu.roll(c, s, 0), 0.0)
        s *= 2
    c = c + carry_ref[...]
    carry_ref[...] = c[tm - 1:tm, :]
    nc = (-LOG2E) * c

    for h in range(fh):
        hs = slice(h * WIDE, h * WIDE + HEAD_DIM)
        xs = slice(h * WIDE + HEAD_DIM, (h + 1) * WIDE)
        fq_ref[:, hs] = (norm(head(0, h), fqg_ref) * QSCALE).astype(BF16)
        fk_ref[:, hs] = norm(head(1, h), fkg_ref).astype(BF16)
        fv_ref[:, hs] = head(2, h).astype(BF16)
        b = jnp.broadcast_to(nc[:, h:h + 1], (tm, LANES))
        hi = b.astype(BF16).astype(F32)
        mid = (b - hi).astype(BF16).astype(F32)
        lo = b - hi - mid
        k_aug = jnp.where(lane == 0, hi, jnp.where(lane == 1, mid, jnp.where(lane == 2, lo,
                                                                             jnp.where(lane == 3, 1.0, 0.0))))
        q_aug = jnp.where(lane < 3, 1.0, jnp.where(lane == 3, -b - hbf_ref[...], 0.0))
        fk_ref[:, xs] = k_aug.astype(BF16)
        fq_ref[:, xs] = q_aug.astype(BF16)
        fv_ref[:, xs] = one_hot0.astype(BF16)
        dq_ref[:, h * LANES:(h + 1) * LANES] = (rope_full(norm(tail("dq", h), dqg_ref)) * QSCALE).astype(BF16)
    dk_ref[:, :HEAD_DIM] = rope_full(norm(tail("dk"), dkg_ref)).astype(BF16)
    dk_ref[:, HEAD_DIM:] = (one_hot0 * (-hbd_ref[...])).astype(BF16)
    dv_ref[:, :HEAD_DIM] = tail("dv").astype(BF16)
    dv_ref[:, HEAD_DIM:] = one_hot0.astype(BF16)

    for j in range(IDX_HEADS // 2):
        blk = rope_idx(tail("iq", j)) * (IDX_DIM ** -0.5)
        iq_ref[2 * j] = blk[:, :IDX_DIM].astype(BF16)
        iq_ref[2 * j + 1] = blk[:, IDX_DIM:].astype(BF16)
    misc = tail("misc")
    ik_ref[...] = rope_idx(misc)[:, :IDX_DIM].astype(BF16)
    iw_ref[...] = misc[:, IDX_DIM:IDX_DIM + IDX_HEADS] * (IDX_HEADS ** -0.5)


def _post(proj_a, proj_b, positions, fb, fqg, fkg, dqg, dkg, half_span_f, half_span_d, D, tm=128):
    S, NB = proj_b.shape
    fh, W, off, total = _tail_layout(D)
    assert total == NB and proj_a.shape == (S, 3 * W)
    tm = min(tm, S)

    def pad_lanes(v):
        return jnp.zeros((1, LANES), F32).at[0, :v.shape[0]].set(v.astype(F32))

    half = HEAD_DIM // 2
    inv_full = ROPE_THETA ** (-jnp.arange(0, HEAD_DIM, 2, dtype=F32) / HEAD_DIM)
    inv1 = jnp.concatenate([inv_full, inv_full]).reshape(1, LANES)
    sgn1 = jnp.concatenate([-jnp.ones(half, F32), jnp.ones(half, F32)]).reshape(1, LANES)
    inv_idx = ROPE_THETA ** (-jnp.arange(0, IDX_DIM, 2, dtype=F32) / IDX_DIM)
    inv2 = jnp.tile(inv_idx, LANES // (IDX_DIM // 2)).reshape(1, LANES)
    q = IDX_DIM // 2
    sgn2 = jnp.tile(jnp.concatenate([-jnp.ones(q, F32), jnp.ones(q, F32)]), LANES // IDX_DIM).reshape(1, LANES)

    row = lambda w: pl.BlockSpec((tm, w), lambda i: (i, 0))
    const = pl.BlockSpec((1, LANES), lambda i: (0, 0))
    wide = jax.ShapeDtypeStruct((S, fh * WIDE), BF16)
    out_shape = (
        wide, wide, wide, jax.ShapeDtypeStruct((S, W), BF16), jax.ShapeDtypeStruct((S, WIDE), BF16),
        jax.ShapeDtypeStruct((S, WIDE), BF16), jax.ShapeDtypeStruct((IDX_HEADS, S, IDX_DIM), BF16),
        jax.ShapeDtypeStruct((S, IDX_DIM), BF16), jax.ShapeDtypeStruct((S, IDX_HEADS), F32),
    )
    out_specs = (
        row(fh * WIDE), row(fh * WIDE), row(fh * WIDE), row(W), row(WIDE), row(WIDE),
        pl.BlockSpec((IDX_HEADS, tm, IDX_DIM), lambda i: (0, i, 0)), row(IDX_DIM), row(IDX_HEADS),
    )
    return pl.pallas_call(
        functools.partial(_post_body, fh=fh, off=off, tm=tm),
        grid=(S // tm,),
        in_specs=[row(3 * W), row(NB), row(1)] + [const] * 11,
        out_specs=out_specs,
        out_shape=out_shape,
        scratch_shapes=[pltpu.VMEM((1, LANES), F32)],
        compiler_params=_params(40, "arbitrary"),
        name="post",
    )(proj_a, proj_b, positions.reshape(S, 1).astype(I32), pad_lanes(fb), fqg.reshape(1, LANES).astype(F32),
      fkg.reshape(1, LANES).astype(F32), dqg.reshape(1, LANES).astype(F32), dkg.reshape(1, LANES).astype(F32),
      inv1, sgn1, inv2, sgn2, jnp.full((1, LANES), half_span_f, F32), jnp.full((1, LANES), half_span_d, F32))


def _attend(s, v, acc_ref, m_ref, track_max):
    if not track_max:
        acc_ref[...] += jnp.dot(jnp.exp2(s).astype(BF16), v, preferred_element_type=F32)
        return
    m_prev = m_ref[...]
    m_new = jnp.maximum(m_prev, jnp.max(s, axis=-1, keepdims=True))
    m_safe = jnp.where(m_new == -jnp.inf, 0.0, m_new)
    alpha = jnp.exp2(m_prev - m_safe)
    p = jnp.exp2(s - jnp.tile(m_safe, (1, s.shape[1] // LANES)))
    acc_ref[...] = jnp.tile(alpha, (1, WIDE // LANES)) * acc_ref[...] + jnp.dot(p.astype(BF16), v,
                                                                                 preferred_element_type=F32)
    m_ref[...] = m_new


def _normalised(acc):
    return acc[:, :HEAD_DIM] / acc[:, HEAD_DIM:HEAD_DIM + 1]


def _logit_span(gq, gk):
    half = LOGIT_BOUND_PER_GAIN * jnp.max(jnp.abs(gq.astype(F32))) * jnp.max(jnp.abs(gk.astype(F32)))
    return 2.0 * half <= SAFE_LOGIT_SPAN, half


def _fox_body(q_ref, k_ref, v_ref, o_ref, acc_ref, m_ref, *, tb, hp, track_max):
    qi = pl.program_id(1)
    acc_ref[...] = jnp.zeros_like(acc_ref)
    if track_max:
        m_ref[...] = jnp.full_like(m_ref, -jnp.inf)

    def step(ki, masked):
        ks = pl.multiple_of(ki * tb, tb)
        for h in range(hp):
            ws = slice(h * WIDE, (h + 1) * WIDE)
            s = lax.dot_general(q_ref[:, ws], k_ref[pl.ds(ks, tb), ws], NT_DIMS, preferred_element_type=F32)
            if masked:
                r = lax.broadcasted_iota(I32, (tb, tb), 0)
                c = lax.broadcasted_iota(I32, (tb, tb), 1)
                s = jnp.where(c <= r, s, -jnp.inf)
            _attend(s, v_ref[pl.ds(ks, tb), ws], acc_ref.at[h], m_ref.at[h], track_max)

    def loop_body(ki, carry):
        step(ki, False)
        return carry

    lax.fori_loop(0, qi, loop_body, 0)
    step(qi, True)
    for h in range(hp):
        o_ref[:, h * HEAD_DIM:(h + 1) * HEAD_DIM] = _normalised(acc_ref[h]).astype(o_ref.dtype)


def _fox_attention(fq, fk, fv, *, track_max, tb=512, hp=4):
    S = fq.shape[0]
    fh = fq.shape[1] // WIDE
    tb = min(tb, S)
    hp = min(hp, fh)
    once = pl.Buffered(1)
    return pl.pallas_call(
        functools.partial(_fox_body, tb=tb, hp=hp, track_max=track_max),
        grid=(fh // hp, S // tb),
        in_specs=[
            pl.BlockSpec((tb, hp * WIDE), lambda h, i: (i, h)),
            pl.BlockSpec((S, hp * WIDE), lambda h, i: (0, h), pipeline_mode=once),
            pl.BlockSpec((S, hp * WIDE), lambda h, i: (0, h), pipeline_mode=once),
        ],
        out_specs=pl.BlockSpec((tb, hp * HEAD_DIM), lambda h, i: (i, h)),
        out_shape=jax.ShapeDtypeStruct((S, fh * HEAD_DIM), BF16),
        scratch_shapes=[pltpu.VMEM((hp, tb, WIDE), F32), pltpu.VMEM((hp, tb, LANES), F32)],
        compiler_params=_params(56, "arbitrary", "arbitrary"),
        name="fox_attention",
    )(fq, fk, fv)


def _sortable_key(x):
    b = lax.bitcast_convert_type(x, I32)
    return b ^ ((b >> 31) & 0x7FFFFFFF)


def _dsa_body(iq_ref, ik_ref, iw_ref, dq_ref, dk_ref, dv_ref, o_ref,
              keys_ref, w_ref, q_ref, acc_ref, m_ref, *, tq, tk, topk, nh, track_max):
    qi = pl.program_id(0)
    groups = max([1] + [g for g in (2, 4, 8) if nh % g == 0 and IDX_HEADS % g == 0 and g * GROUP_ROWS <= nh * tq])
    ig, ag = IDX_HEADS // groups, nh // groups
    nkb = (qi * tq) // tk + 1
    row_g = qi * tq + lax.broadcasted_iota(I32, (tq, tk), 0)
    col_l = lax.broadcasted_iota(I32, (tq, tk), 1)

    for h in range(IDX_HEADS):
        w_ref[h * tq:(h + 1) * tq, :] = iw_ref[:, h:h + 1]

    def score_chunk(kb, carry):
        ks = pl.multiple_of(kb * tk, tk)
        ikb = ik_ref[pl.ds(ks, tk), :]
        sc = None
        for g in range(groups):
            rows = slice(g * ig * tq, (g + 1) * ig * tq)
            iq_g = iq_ref[g * ig:(g + 1) * ig].reshape(ig * tq, IDX_DIM)
            logits = lax.dot_general(iq_g, ikb, NT_DIMS, preferred_element_type=F32)
            part = jnp.sum((jnp.maximum(logits, 0.0) * w_ref[rows, :]).reshape(ig, tq, tk), axis=0)
            sc = part if sc is None else sc + part
        sc = jnp.where(ks + col_l <= row_g, sc, -jnp.inf)
        keys_ref[kb] = _sortable_key(sc)
        return carry

    lax.fori_loop(0, nkb, score_chunk, 0)

    def bit_step(i, thr):
        cand = thr + lax.shift_left(jnp.int32(1), 31 - i)

        cand_rows = [jnp.broadcast_to(cand[r:r + COUNT_ROWS], (COUNT_ROWS, LANES)) for r in range(0, tq, COUNT_ROWS)]

        def count_chunk(kb, cnts):
            out = []
            for n, r in enumerate(range(0, tq, COUNT_ROWS)):
                c = cnts[n]
                for j in range(tk // LANES):
                    k = keys_ref[kb, r:r + COUNT_ROWS, j * LANES:(j + 1) * LANES]
                    c = c + jnp.where(k >= cand_rows[n], 1.0, 0.0)
                out.append(c)
            return tuple(out)

        zero = jnp.zeros((COUNT_ROWS, LANES), F32)
        cnts = lax.fori_loop(0, nkb, count_chunk, (zero,) * (tq // COUNT_ROWS))
        cnt = jnp.sum(jnp.concatenate(cnts, axis=0), axis=-1, keepdims=True)
        return jnp.where(cnt >= float(topk), cand, thr)

    thr = lax.fori_loop(0, 32, bit_step, jnp.full((tq, 1), INT32_MIN, I32))
    thr = jnp.maximum(thr, NEG_INF_KEY + 1)

    lane = lax.broadcasted_iota(I32, (nh * tq, LANES), 1)
    q_ref[:, HEAD_DIM:] = jnp.where(lane == 0, 1.0, 0.0).astype(BF16)
    for h in range(nh):
        q_ref[h * tq:(h + 1) * tq, :HEAD_DIM] = dq_ref[:, h * HEAD_DIM:(h + 1) * HEAD_DIM]
    acc_ref[...] = jnp.zeros_like(acc_ref)
    if track_max:
        m_ref[...] = jnp.full_like(m_ref, -jnp.inf)

    def attn_chunk(kb, carry):
        ks = pl.multiple_of(kb * tk, tk)
        k = dk_ref[pl.ds(ks, tk), :]
        v = dv_ref[pl.ds(ks, tk), :]
        drop = jnp.where(keys_ref[kb] >= thr, 0.0, -jnp.inf)
        for g in range(groups):
            rows = pl.ds(g * ag * tq, ag * tq)
            s = lax.dot_general(q_ref[rows, :], k, NT_DIMS, preferred_element_type=F32)
            s = (s.reshape(ag, tq, tk) + drop[None]).reshape(ag * tq, tk)
            _attend(s, v, acc_ref.at[rows], m_ref.at[rows], track_max)
        return carry

    lax.fori_loop(0, nkb, attn_chunk, 0)
    out = _normalised(acc_ref[...])
    for h in range(nh):
        o_ref[:, h * HEAD_DIM:(h + 1) * HEAD_DIM] = out[h * tq:(h + 1) * tq, :].astype(o_ref.dtype)


GROUP_ROWS = 1024
COUNT_ROWS = 64


def _dsa_attention(iq, ik, iw, dq, dk, dv, *, topk, track_max, tq=256, tk=512):
    S, W = dq.shape
    nh = W // HEAD_DIM
    tq, tk = min(tq, S), min(tk, S)
    nkb = S // tk
    full = lambda shape: pl.BlockSpec(shape, lambda i: (0,) * len(shape))
    return pl.pallas_call(
        functools.partial(_dsa_body, tq=tq, tk=tk, topk=topk, nh=nh, track_max=track_max),
        grid=(S // tq,),
        in_specs=[
            pl.BlockSpec((IDX_HEADS, tq, IDX_DIM), lambda i: (0, i, 0)),
            full((S, IDX_DIM)),
            pl.BlockSpec((tq, IDX_HEADS), lambda i: (i, 0)),
            pl.BlockSpec((tq, W), lambda i: (i, 0)),
            full((S, WIDE)),
            full((S, WIDE)),
        ],
        out_specs=pl.BlockSpec((tq, W), lambda i: (i, 0)),
        out_shape=jax.ShapeDtypeStruct((S, W), BF16),
        scratch_shapes=[
            pltpu.VMEM((nkb, tq, tk), I32),
            pltpu.VMEM((IDX_HEADS * tq, 1), F32),
            pltpu.VMEM((nh * tq, WIDE), BF16),
            pltpu.VMEM((nh * tq, WIDE), F32),
            pltpu.VMEM((nh * tq, LANES), F32),
        ],
        compiler_params=_params(58, "arbitrary"),
        name="dsa_attention",
    )(iq, ik, iw, dq, dk, dv)


ROUTE_GROUP = 8


NOT_RANKED = float(PEER_NKEYS - 1)


def _top_rows(s, n, with_rank=False):
    rows = []
    rank = jnp.full(s.shape, NOT_RANKED, F32)
    for k in range(n):
        m = jnp.max(s, axis=0, keepdims=True)
        rows.append(m)
        hit = s == m
        if with_rank:
            rank = jnp.where(hit, float(k), rank)
        s = jnp.where(hit, -jnp.inf, s)
    return (rows, rank) if with_rank else rows


def _route_body(qt_ref, kk_ref, n1_ref, e1_ref, r2_ref, e2_ref):
    ts = qt_ref.shape[1]
    ng = PEER_NKEYS // ROUTE_GROUP
    for h in range(PEER_HEADS):
        qh = qt_ref[h * LANES:(h + 1) * LANES, :]
        sc = jnp.dot(kk_ref[h], qh, preferred_element_type=F32, precision=lax.Precision.HIGHEST)
        s1, s2 = sc[:PEER_NKEYS], sc[PEER_NKEYS:]
        v1, rank1 = _top_rows(s1, PEER_TOPK, with_rank=True)
        v2, rank2 = _top_rows(s2, PEER_TOPK, with_rank=True)
        v2 = jnp.concatenate(v2, axis=0)
        cand = [v1[a] + v2[:-(-(PEER_TOPK // (a + 1)) // 8) * 8] for a in range(PEER_TOPK)]
        best = _top_rows(jnp.concatenate(cand, axis=0), PEER_TOPK)
        tau = best[PEER_TOPK - 1]
        z = sum(jnp.exp(b - best[0]) for b in best)
        n1 = jnp.zeros((PEER_NKEYS, ts), F32)
        for a in range(PEER_TOPK):
            n_a = jnp.sum(jnp.where(cand[a] >= tau, 1.0, 0.0), axis=0, keepdims=True)
            n1 = jnp.where(rank1 == float(a), n_a, n1)
        n1_ref[:, h] = n1.reshape(ng, ROUTE_GROUP, ts)
        e1_ref[:, h] = (jnp.exp(s1 - v1[0]) / z).reshape(ng, ROUTE_GROUP, ts)
        r2_ref[h] = rank2.astype(BF16)
        e2_ref[h] = jnp.exp(s2 - v2[0:1]).astype(BF16)


def _peer_route(qt, kk, ts=256):
    NQ, S = qt.shape
    ts = min(ts, S)
    ng = PEER_NKEYS // ROUTE_GROUP
    grouped = jax.ShapeDtypeStruct((ng, PEER_HEADS, ROUTE_GROUP, S), F32)
    whole = jax.ShapeDtypeStruct((PEER_HEADS, PEER_NKEYS, S), BF16)
    gspec = pl.BlockSpec((ng, PEER_HEADS, ROUTE_GROUP, ts), lambda i: (0, 0, 0, i))
    wspec = pl.BlockSpec((PEER_HEADS, PEER_NKEYS, ts), lambda i: (0, 0, i))
    return pl.pallas_call(
        _route_body,
        grid=(S // ts,),
        in_specs=[pl.BlockSpec((NQ, ts), lambda i: (0, i)),
                  pl.BlockSpec((PEER_HEADS, 2 * PEER_NKEYS, LANES), lambda i: (0, 0, 0))],
        out_specs=(gspec, gspec, wspec, wspec),
        out_shape=(grouped, grouped, whole, whole),
        compiler_params=_params(40, "arbitrary"),
        name="peer_route",
    )(qt, kk)


def _experts_body(x_ref, u_ref, v_ref, n1_ref, e1_ref, r2_ref, e2_ref, res_ref, o_ref, *, nsub):
    eb = pl.program_id(1)

    @pl.when(eb == 0)
    def _():
        o_ref[...] = jnp.zeros_like(o_ref)

    rr = res_ref.shape[0]
    r0 = pl.multiple_of(eb * rr, rr)
    o_ref[pl.ds(r0, rr), :] += res_ref[...]

    first = (eb * nsub) % ROUTE_GROUP
    tb = x_ref.shape[0]
    ht = lax.dot_general(u_ref[...], x_ref[...], NT_DIMS, preferred_element_type=F32)
    parts = []
    reps = PEER_NKEYS // BF16_ROWS

    def row_tile(ref, h, row):
        packed = jnp.broadcast_to(ref[h, pl.ds(row, 1), :], (BF16_ROWS, tb)).astype(BF16)
        return jnp.tile(packed, (reps, 1))

    for j in range(nsub):
        gate = jnp.zeros((PEER_NKEYS, tb), BF16)
        for h in range(PEER_HEADS):
            routed = r2_ref[h] < row_tile(n1_ref, h, first + j)
            gate = gate + jnp.where(routed, e2_ref[h] * row_tile(e1_ref, h, first + j), 0.0)
        hj = ht[j * PEER_NKEYS:(j + 1) * PEER_NKEYS, :]
        act = 0.5 * hj * (1.0 + lax.erf(hj * (2.0 ** -0.5)))
        parts.append(gate.astype(F32) * act)
    a = jnp.concatenate(parts, axis=0).T.astype(BF16)
    o_ref[...] += jnp.dot(a, v_ref[...], preferred_element_type=F32)


BF16_ROWS = 16


def _peer_experts(h2, u, v, n1, e1, r2, e2, res, tb=512, te=1024):
    S, D = h2.shape
    E = u.shape[0]
    tb = min(tb, S)
    nsub = te // PEER_NKEYS
    ne = E // te
    rr = tb // ne
    assert ROUTE_GROUP % nsub == 0 and tb % ne == 0 and rr % 8 == 0
    grouped = pl.BlockSpec((None, PEER_HEADS, ROUTE_GROUP, tb), lambda i, j: ((j * nsub) // ROUTE_GROUP, 0, 0, i))
    once = pl.Buffered(1)
    whole = pl.BlockSpec((PEER_HEADS, PEER_NKEYS, tb), lambda i, j: (0, 0, i), pipeline_mode=once)
    return pl.pallas_call(
        functools.partial(_experts_body, nsub=nsub),
        grid=(S // tb, ne),
        in_specs=[
            pl.BlockSpec((tb, D), lambda i, j: (i, 0), pipeline_mode=once),
            pl.BlockSpec((te, D), lambda i, j: (j, 0)),
            pl.BlockSpec((te, D), lambda i, j: (j, 0)),
            grouped, grouped, whole, whole,
            pl.BlockSpec((rr, D), lambda i, j: (i * ne + j, 0)),
        ],
        out_specs=pl.BlockSpec((tb, D), lambda i, j: (i, 0), pipeline_mode=once),
        out_shape=jax.ShapeDtypeStruct((S, D), F32),
        compiler_params=_params(60, "arbitrary", "arbitrary"),
        name="peer_experts",
    )(h2, u, v, n1, e1, r2, e2, res)


def _relayout_w_tail(w_in, D):
    fh, W, off, total = _tail_layout(D)
    sizes = (fh, W, HEAD_DIM, HEAD_DIM, IDX_HEADS * IDX_DIM, IDX_DIM, IDX_HEADS)
    pts = (3 * W + np.cumsum(sizes)[:-1]).tolist()
    _, ff, dq, dk, dv, iq, ik, iw = jnp.split(w_in, [3 * W] + pts, axis=1)
    zeros = lambda n: jnp.zeros((D, n), w_in.dtype)
    misc = jnp.concatenate([ik, iw, zeros(LANES - IDX_DIM - IDX_HEADS)], axis=1)
    ffp = jnp.concatenate([ff, zeros(LANES - fh)], axis=1)
    cols = [dq, iq, dk, dv, misc, ffp]
    used = sum(c.shape[1] for c in cols)
    if total > used:
        cols.append(zeros(total - used))
    return jnp.concatenate(cols, axis=1).astype(BF16)


def _layer(x, positions, ln1_g, w_in, fox_forget_b, fox_qn_g, fox_kn_g, dsa_qn_g, dsa_kn_g,
           w_o, ln2_g, peer_wq, peer_keys1, peer_keys2, peer_u, peer_v):
    S, D = x.shape
    W = D // 2
    topk = min(DSA_TOPK_MAX, S // 4)

    h1 = _rmsnorm(x, ln1_g, BF16)
    proj_a = _matmul(h1, w_in[:, :3 * W].astype(BF16), name="in_proj_fox")
    proj_b = _matmul(h1, _relayout_w_tail(w_in, D), name="in_proj_dsa")
    fox_safe, fox_half = _logit_span(fox_qn_g, fox_kn_g)
    dsa_safe, dsa_half = _logit_span(dsa_qn_g, dsa_kn_g)
    fq, fk, fv, dq, dk, dv, iq, ik, iw = _post(proj_a, proj_b, positions, fox_forget_b, fox_qn_g, fox_kn_g,
                                               dsa_qn_g, dsa_kn_g, fox_half, dsa_half, D)
    fox_out = lax.cond(fox_safe, functools.partial(_fox_attention, track_max=False),
                       functools.partial(_fox_attention, track_max=True), fq, fk, fv)
    dsa_out = lax.cond(dsa_safe, functools.partial(_dsa_attention, topk=topk, track_max=False),
                       functools.partial(_dsa_attention, topk=topk, track_max=True), iq, ik, iw, dq, dk, dv)
    x1 = _out_proj(fox_out, dsa_out, w_o, x)

    h2 = _rmsnorm(x1, ln2_g, BF16)
    qt = _matmul(peer_wq.T.astype(BF16), h2, nt=True, name="peer_query")
    zk = jnp.zeros_like(peer_keys1)
    kk = jnp.concatenate([jnp.concatenate([peer_keys1, zk], axis=2),
                          jnp.concatenate([zk, peer_keys2], axis=2)], axis=1).astype(F32)
    n1, e1, r2, e2 = _peer_route(qt, kk)
    return _peer_experts(h2, peer_u.astype(BF16), peer_v.astype(BF16), n1, e1, r2, e2, x1)


def kernel(x, positions, ln1_g, w_in, fox_forget_b, fox_qn_g, fox_kn_g, dsa_qn_g, dsa_kn_g, w_o, ln2_g, peer_wq,
           peer_keys1, peer_keys2, peer_u, peer_v):
    B = x.shape[0]
    depth = w_in.shape[0]
    outs = []
    for b in range(B):
        xb = x[b]
        for l in range(depth):
            xb = _layer(xb, positions[b], ln1_g[l], w_in[l], fox_forget_b[l], fox_qn_g[l], fox_kn_g[l],
                        dsa_qn_g[l], dsa_kn_g[l], w_o[l], ln2_g[l], peer_wq[l], peer_keys1[l], peer_keys2[l],
                        peer_u[l], peer_v[l])
        outs.append(xb)
    return outs[0][None] if B == 1 else jnp.stack(outs, axis=0)
```

```python
import functools

import numpy as np
import jax
import jax.numpy as jnp
from jax import lax
from jax.experimental import pallas as pl
from jax.experimental.pallas import tpu as pltpu

F32 = jnp.float32
BF16 = jnp.bfloat16
I32 = jnp.int32

HEAD_DIM = 128
IDX_HEADS = 16
IDX_DIM = 64
DSA_TOPK_MAX = 256
ROPE_THETA = 10000.0
NORM_EPS = 1e-6
PEER_HEADS = 8
PEER_NKEYS = 128
PEER_SUBDIM = 64
PEER_TOPK = 16
LOG2E = 1.4426950408889634
LANES = 128
WIDE = 2 * HEAD_DIM
QSCALE = (HEAD_DIM ** -0.5) * LOG2E

LOGIT_BOUND_PER_GAIN = (HEAD_DIM ** 0.5) * LOG2E
SAFE_LOGIT_SPAN = 60.0

NEG_INF_KEY = -2139095041
INT32_MIN = -2147483648

NT_DIMS = (((1,), (1,)), ((), ()))
NN_DIMS = (((1,), (0,)), ((), ()))


def _params(vmem_mb, *sem):
    return pltpu.CompilerParams(dimension_semantics=sem, vmem_limit_bytes=vmem_mb * 1024 * 1024)


def _as_bf16(x):
    return x if x.dtype == BF16 else x.astype(BF16)


def _rmsnorm_body(x_ref, g_ref, o_ref):
    x = x_ref[...].astype(F32)
    y = x * lax.rsqrt(jnp.mean(x * x, axis=-1, keepdims=True) + NORM_EPS)
    o_ref[...] = (y * g_ref[...]).astype(o_ref.dtype)


def _rmsnorm(x2d, g, out_dtype, tm=256):
    S, D = x2d.shape
    tm = min(tm, S)
    return pl.pallas_call(
        _rmsnorm_body,
        grid=(S // tm,),
        in_specs=[pl.BlockSpec((tm, D), lambda i: (i, 0)), pl.BlockSpec((1, D), lambda i: (0, 0))],
        out_specs=pl.BlockSpec((tm, D), lambda i: (i, 0)),
        out_shape=jax.ShapeDtypeStruct((S, D), out_dtype),
        compiler_params=_params(40, "arbitrary"),
        name="rmsnorm",
    )(x2d, g.reshape(1, D).astype(F32))


def _matmul_body(a_ref, b_ref, o_ref, *, nt):
    acc = lax.dot_general(a_ref[...], _as_bf16(b_ref[...]), NT_DIMS if nt else NN_DIMS, preferred_element_type=F32)
    o_ref[...] = acc.astype(o_ref.dtype)


def _matmul(a, b, *, nt=False, n_cols=None, out_dtype=F32, tm=1024, tn=512, name="matmul"):
    M, K = a.shape
    N = n_cols if n_cols is not None else (b.shape[0] if nt else b.shape[1])
    tm = min(tm, M)
    tn = next(t for t in range(min(tn, N), 0, -LANES) if N % t == 0)
    assert M % tm == 0 and tn % LANES == 0
    return pl.pallas_call(
        functools.partial(_matmul_body, nt=nt),
        grid=(M // tm, N // tn),
        in_specs=[
            pl.BlockSpec((tm, K), lambda i, j: (i, 0)),
            pl.BlockSpec((tn, K), lambda i, j: (j, 0)) if nt else pl.BlockSpec((K, tn), lambda i, j: (0, j)),
        ],
        out_specs=pl.BlockSpec((tm, tn), lambda i, j: (i, j)),
        out_shape=jax.ShapeDtypeStruct((M, N), out_dtype),
        compiler_params=_params(48, "arbitrary", "arbitrary"),
        name=name,
    )(a, b)


def _out_proj_body(a1_ref, a2_ref, b1_ref, b2_ref, r_ref, o_ref):
    acc = jnp.dot(a1_ref[...], _as_bf16(b1_ref[...]), preferred_element_type=F32)
    acc = acc + jnp.dot(a2_ref[...], _as_bf16(b2_ref[...]), preferred_element_type=F32)
    o_ref[...] = acc + r_ref[...]


def _out_proj(a1, a2, w, res, tm=1024, tn=512):
    M, K1 = a1.shape
    N = w.shape[1]
    assert a2.shape == a1.shape and w.shape[0] == 2 * K1
    tm, tn = min(tm, M), min(tn, N)
    lhs = pl.BlockSpec((tm, K1), lambda i, j: (i, 0))
    return pl.pallas_call(
        _out_proj_body,
        grid=(M // tm, N // tn),
        in_specs=[lhs, lhs, pl.BlockSpec((K1, tn), lambda i, j: (0, j)), pl.BlockSpec((K1, tn), lambda i, j: (1, j)),
                  pl.BlockSpec((tm, tn), lambda i, j: (i, j))],
        out_specs=pl.BlockSpec((tm, tn), lambda i, j: (i, j)),
        out_shape=jax.ShapeDtypeStruct((M, N), F32),
        compiler_params=_params(48, "arbitrary", "arbitrary"),
        name="out_proj",
    )(a1, a2, w, w, res)


def _tail_layout(D):
    fh = D // (2 * HEAD_DIM)
    W = fh * HEAD_DIM
    off = {}
    pos = 0
    for name, width in (("dq", W), ("iq", IDX_HEADS * IDX_DIM), ("dk", HEAD_DIM), ("dv", HEAD_DIM),
                        ("misc", LANES), ("ff", LANES)):
        off[name] = pos
        pos += width
    total = -(-pos // 512) * 512
    return fh, W, off, total


def _post_body(pa_ref, pb_ref, pos_ref, fb_ref, fqg_ref, fkg_ref, dqg_ref, dkg_ref, inv1_ref, sgn1_ref, inv2_ref,
               sgn2_ref, hbf_ref, hbd_ref,
               fq_ref, fk_ref, fv_ref, dq_ref, dk_ref, dv_ref, iq_ref, ik_ref, iw_ref, carry_ref,
               *, fh, off, tm):
    @pl.when(pl.program_id(0) == 0)
    def _():
        carry_ref[...] = jnp.zeros_like(carry_ref)

    W = fh * HEAD_DIM
    pos = pos_ref[...].astype(F32)
    ang1 = pos * inv1_ref[...]
    cos1, sin1 = jnp.cos(ang1), jnp.sin(ang1) * sgn1_ref[...]
    ang2 = pos * inv2_ref[...]
    cos2, sin2 = jnp.cos(ang2), jnp.sin(ang2) * sgn2_ref[...]
    lane = lax.broadcasted_iota(I32, (tm, LANES), 1)
    lo_half = (lane & (IDX_DIM - 1)) < (IDX_DIM // 2)
    one_hot0 = jnp.where(lane == 0, 1.0, 0.0)

    def head(seg, h):
        s = seg * W + h * LANES
        return pa_ref[:, s:s + LANES]

    def tail(name, j=0):
        s = off[name] + j * LANES
        return pb_ref[:, s:s + LANES]

    def norm(xh, g_ref):
        return xh * lax.rsqrt(jnp.mean(xh * xh, axis=-1, keepdims=True) + NORM_EPS) * g_ref[...]

    def rope_full(xh):
        return xh * cos1 + pltpu.roll(xh, HEAD_DIM // 2, 1) * sin1

    def rope_idx(xh):
        r = jnp.where(lo_half, pltpu.roll(xh, LANES - IDX_DIM // 2, 1), pltpu.roll(xh, IDX_DIM // 2, 1))
        return xh * cos2 + r * sin2

    z = tail("ff") + fb_ref[...]
    c = jnp.minimum(z, 0.0) - jnp.log1p(jnp.exp(-jnp.abs(z)))
    row = lax.broadcasted_iota(I32, (tm, LANES), 0)
    s = 1
    while s < tm:
        c = c + jnp.where(row >= s, pltpu.roll(c, s, 0), 0.0)
        s *= 2
    c = c + carry_ref[...]
    carry_ref[...] = c[tm - 1:tm, :]
    nc = (-LOG2E) * c

    for h in range(fh):
        hs = slice(h * WIDE, h * WIDE + HEAD_DIM)
        xs = slice(h * WIDE + HEAD_DIM, (h + 1) * WIDE)
        fq_ref[:, hs] = (norm(head(0, h), fqg_ref) * QSCALE).astype(BF16)
        fk_ref[:, hs] = norm(head(1, h), fkg_ref).astype(BF16)
        fv_ref[:, hs] = head(2, h).astype(BF16)
        b = jnp.broadcast_to(nc[:, h:h + 1], (tm, LANES))
        hi = b.astype(BF16).astype(F32)
        mid = (b - hi).astype(BF16).astype(F32)
        lo = b - hi - mid
        k_aug = jnp.where(lane == 0, hi, jnp.where(lane == 1, mid, jnp.where(lane == 2, lo,
                                                                             jnp.where(lane == 3, 1.0, 0.0))))
        q_aug = jnp.where(lane < 3, 1.0, jnp.where(lane == 3, -b - hbf_ref[...], 0.0))
        fk_ref[:, xs] = k_aug.astype(BF16)
        fq_ref[:, xs] = q_aug.astype(BF16)
        fv_ref[:, xs] = one_hot0.astype(BF16)
        dq_ref[:, h * LANES:(h + 1) * LANES] = (rope_full(norm(tail("dq", h), dqg_ref)) * QSCALE).astype(BF16)
    dk_ref[:, :HEAD_DIM] = rope_full(norm(tail("dk"), dkg_ref)).astype(BF16)
    dk_ref[:, HEAD_DIM:] = (one_hot0 * (-hbd_ref[...])).astype(BF16)
    dv_ref[:, :HEAD_DIM] = tail("dv").astype(BF16)
    dv_ref[:, HEAD_DIM:] = one_hot0.astype(BF16)

    for j in range(IDX_HEADS // 2):
        blk = rope_idx(tail("iq", j)) * (IDX_DIM ** -0.5)
        iq_ref[2 * j] = blk[:, :IDX_DIM].astype(BF16)
        iq_ref[2 * j + 1] = blk[:, IDX_DIM:].astype(BF16)
    misc = tail("misc")
    ik_ref[...] = rope_idx(misc)[:, :IDX_DIM].astype(BF16)
    iw_ref[...] = misc[:, IDX_DIM:IDX_DIM + IDX_HEADS] * (IDX_HEADS ** -0.5)


def _post(proj_a, proj_b, positions, fb, fqg, fkg, dqg, dkg, half_span_f, half_span_d, D, tm=128):
    S, NB = proj_b.shape
    fh, W, off, total = _tail_layout(D)
    assert total == NB and proj_a.shape == (S, 3 * W)
    tm = min(tm, S)

    def pad_lanes(v):
        return jnp.zeros((1, LANES), F32).at[0, :v.shape[0]].set(v.astype(F32))

    half = HEAD_DIM // 2
    inv_full = ROPE_THETA ** (-jnp.arange(0, HEAD_DIM, 2, dtype=F32) / HEAD_DIM)
    inv1 = jnp.concatenate([inv_full, inv_full]).reshape(1, LANES)
    sgn1 = jnp.concatenate([-jnp.ones(half, F32), jnp.ones(half, F32)]).reshape(1, LANES)
    inv_idx = ROPE_THETA ** (-jnp.arange(0, IDX_DIM, 2, dtype=F32) / IDX_DIM)
    inv2 = jnp.tile(inv_idx, LANES // (IDX_DIM // 2)).reshape(1, LANES)
    q = IDX_DIM // 2
    sgn2 = jnp.tile(jnp.concatenate([-jnp.ones(q, F32), jnp.ones(q, F32)]), LANES // IDX_DIM).reshape(1, LANES)

    row = lambda w: pl.BlockSpec((tm, w), lambda i: (i, 0))
    const = pl.BlockSpec((1, LANES), lambda i: (0, 0))
    wide = jax.ShapeDtypeStruct((S, fh * WIDE), BF16)
    out_shape = (
        wide, wide, wide, jax.ShapeDtypeStruct((S, W), BF16), jax.ShapeDtypeStruct((S, WIDE), BF16),
        jax.ShapeDtypeStruct((S, WIDE), BF16), jax.ShapeDtypeStruct((IDX_HEADS, S, IDX_DIM), BF16),
        jax.ShapeDtypeStruct((S, IDX_DIM), BF16), jax.ShapeDtypeStruct((S, IDX_HEADS), F32),
    )
    out_specs = (
        row(fh * WIDE), row(fh * WIDE), row(fh * WIDE), row(W), row(WIDE), row(WIDE),
        pl.BlockSpec((IDX_HEADS, tm, IDX_DIM), lambda i: (0, i, 0)), row(IDX_DIM), row(IDX_HEADS),
    )
    return pl.pallas_call(
        functools.partial(_post_body, fh=fh, off=off, tm=tm),
        grid=(S // tm,),
        in_specs=[row(3 * W), row(NB), row(1)] + [const] * 11,
        out_specs=out_specs,
        out_shape=out_shape,
        scratch_shapes=[pltpu.VMEM((1, LANES), F32)],
        compiler_params=_params(40, "arbitrary"),
        name="post",
    )(proj_a, proj_b, positions.reshape(S, 1).astype(I32), pad_lanes(fb), fqg.reshape(1, LANES).astype(F32),
      fkg.reshape(1, LANES).astype(F32), dqg.reshape(1, LANES).astype(F32), dkg.reshape(1, LANES).astype(F32),
      inv1, sgn1, inv2, sgn2, jnp.full((1, LANES), half_span_f, F32), jnp.full((1, LANES), half_span_d, F32))


def _attend(s, v, acc_ref, m_ref, track_max):
    if not track_max:
        acc_ref[...] += jnp.dot(jnp.exp2(s).astype(BF16), v, preferred_element_type=F32)
        return
    m_prev = m_ref[...]
    m_new = jnp.maximum(m_prev, jnp.max(s, axis=-1, keepdims=True))
    m_safe = jnp.where(m_new == -jnp.inf, 0.0, m_new)
    alpha = jnp.exp2(m_prev - m_safe)
    p = jnp.exp2(s - jnp.tile(m_safe, (1, s.shape[1] // LANES)))
    acc_ref[...] = jnp.tile(alpha, (1, WIDE // LANES)) * acc_ref[...] + jnp.dot(p.astype(BF16), v,
                                                                                 preferred_element_type=F32)
    m_ref[...] = m_new


def _normalised(acc):
    return acc[:, :HEAD_DIM] / acc[:, HEAD_DIM:HEAD_DIM + 1]


def _logit_span(gq, gk):
    half = LOGIT_BOUND_PER_GAIN * jnp.max(jnp.abs(gq.astype(F32))) * jnp.max(jnp.abs(gk.astype(F32)))
    return 2.0 * half <= SAFE_LOGIT_SPAN, half


def _fox_body(q_ref, k_ref, v_ref, o_ref, acc_ref, m_ref, *, tb, hp, track_max):
    qi = pl.program_id(1)
    acc_ref[...] = jnp.zeros_like(acc_ref)
    if track_max:
        m_ref[...] = jnp.full_like(m_ref, -jnp.inf)

    def step(ki, masked):
        ks = pl.multiple_of(ki * tb, tb)
        for h in range(hp):
            ws = slice(h * WIDE, (h + 1) * WIDE)
            s = lax.dot_general(q_ref[:, ws], k_ref[pl.ds(ks, tb), ws], NT_DIMS, preferred_element_type=F32)
            if masked:
                r = lax.broadcasted_iota(I32, (tb, tb), 0)
                c = lax.broadcasted_iota(I32, (tb, tb), 1)
                s = jnp.where(c <= r, s, -jnp.inf)
            _attend(s, v_ref[pl.ds(ks, tb), ws], acc_ref.at[h], m_ref.at[h], track_max)

    def loop_body(ki, carry):
        step(ki, False)
        return carry

    lax.fori_loop(0, qi, loop_body, 0)
    step(qi, True)
    for h in range(hp):
        o_ref[:, h * HEAD_DIM:(h + 1) * HEAD_DIM] = _normalised(acc_ref[h]).astype(o_ref.dtype)


def _fox_attention(fq, fk, fv, *, track_max, tb=512, hp=4):
    S = fq.shape[0]
    fh = fq.shape[1] // WIDE
    tb = min(tb, S)
    hp = min(hp, fh)
    once = pl.Buffered(1)
    return pl.pallas_call(
        functools.partial(_fox_body, tb=tb, hp=hp, track_max=track_max),
        grid=(fh // hp, S // tb),
        in_specs=[
            pl.BlockSpec((tb, hp * WIDE), lambda h, i: (i, h)),
            pl.BlockSpec((S, hp * WIDE), lambda h, i: (0, h), pipeline_mode=once),
            pl.BlockSpec((S, hp * WIDE), lambda h, i: (0, h), pipeline_mode=once),
        ],
        out_specs=pl.BlockSpec((tb, hp * HEAD_DIM), lambda h, i: (i, h)),
        out_shape=jax.ShapeDtypeStruct((S, fh * HEAD_DIM), BF16),
        scratch_shapes=[pltpu.VMEM((hp, tb, WIDE), F32), pltpu.VMEM((hp, tb, LANES), F32)],
        compiler_params=_params(56, "arbitrary", "arbitrary"),
        name="fox_attention",
    )(fq, fk, fv)


def _sortable_key(x):
    b = lax.bitcast_convert_type(x, I32)
    return b ^ ((b >> 31) & 0x7FFFFFFF)


def _coarse_float(key):
    kh = key & jnp.int32(-65536)
    return lax.bitcast_convert_type(kh ^ ((kh >> 31) & 0x7FFF0000), F32)


def _dsa_body(iq_ref, ik_ref, iw_ref, dq_ref, dk_ref, dv_ref, o_ref,
              keys_ref, hi_ref, w_ref, q_ref, acc_ref, m_ref, *, tq, tk, topk, nh, track_max):
    qi = pl.program_id(0)
    groups = max([1] + [g for g in (2, 4, 8) if nh % g == 0 and IDX_HEADS % g == 0 and g * GROUP_ROWS <= nh * tq])
    ig, ag = IDX_HEADS // groups, nh // groups
    nkb = (qi * tq) // tk + 1
    row_g = qi * tq + lax.broadcasted_iota(I32, (tq, tk), 0)
    col_l = lax.broadcasted_iota(I32, (tq, tk), 1)

    for h in range(IDX_HEADS):
        w_ref[h * tq:(h + 1) * tq, :] = iw_ref[:, h:h + 1]

    def score_chunk(kb, carry):
        ks = pl.multiple_of(kb * tk, tk)
        ikb = ik_ref[pl.ds(ks, tk), :]
        sc = None
        for g in range(groups):
            rows = slice(g * ig * tq, (g + 1) * ig * tq)
            iq_g = iq_ref[g * ig:(g + 1) * ig].reshape(ig * tq, IDX_DIM)
            logits = lax.dot_general(iq_g, ikb, NT_DIMS, preferred_element_type=F32)
            part = jnp.sum((jnp.maximum(logits, 0.0) * w_ref[rows, :]).reshape(ig, tq, tk), axis=0)
            sc = part if sc is None else sc + part
        sc = jnp.where(ks + col_l <= row_g, sc, -jnp.inf)
        key = _sortable_key(sc)
        keys_ref[kb] = key
        hi_ref[kb] = _coarse_float(key).astype(BF16)
        return carry

    lax.fori_loop(0, nkb, score_chunk, 0)

    def bit_step(i, thr, coarse):
        cand = thr + lax.shift_left(jnp.int32(1), 31 - i)
        src, dt = (hi_ref, BF16) if coarse else (keys_ref, F32)
        cmp = _coarse_float(cand).astype(BF16) if coarse else cand
        cand_rows = [jnp.broadcast_to(cmp[r:r + COUNT_ROWS], (COUNT_ROWS, LANES)) for r in range(0, tq, COUNT_ROWS)]
        one, zero = jnp.ones((), dt), jnp.zeros((), dt)

        def count_chunk(kb, cnts):
            out = []
            for n, r in enumerate(range(0, tq, COUNT_ROWS)):
                c = cnts[n]
                for j in range(tk // LANES):
                    k = src[kb, r:r + COUNT_ROWS, j * LANES:(j + 1) * LANES]
                    c = c + jnp.where(k >= cand_rows[n], one, zero)
                out.append(c)
            return tuple(out)

        start = jnp.zeros((COUNT_ROWS, LANES), dt)
        cnts = lax.fori_loop(0, nkb, count_chunk, (start,) * (tq // COUNT_ROWS))
        cnt = jnp.sum(jnp.concatenate(cnts, axis=0).astype(F32), axis=-1, keepdims=True)
        return jnp.where(cnt >= float(topk), cand, thr)

    thr = bit_step(0, jnp.full((tq, 1), INT32_MIN, I32), False)
    thr = lax.fori_loop(1, 16, functools.partial(bit_step, coarse=True), thr)
    thr = lax.fori_loop(16, 32, functools.partial(bit_step, coarse=False), thr)
    thr = jnp.maximum(thr, NEG_INF_KEY + 1)

    lane = lax.broadcasted_iota(I32, (nh * tq, LANES), 1)
    q_ref[:, HEAD_DIM:] = jnp.where(lane == 0, 1.0, 0.0).astype(BF16)
    for h in range(nh):
        q_ref[h * tq:(h + 1) * tq, :HEAD_DIM] = dq_ref[:, h * HEAD_DIM:(h + 1) * HEAD_DIM]
    acc_ref[...] = jnp.zeros_like(acc_ref)
    if track_max:
        m_ref[...] = jnp.full_like(m_ref, -jnp.inf)

    def attn_chunk(kb, carry):
        ks = pl.multiple_of(kb * tk, tk)
        k = dk_ref[pl.ds(ks, tk), :]
        v = dv_ref[pl.ds(ks, tk), :]
        drop = jnp.where(keys_ref[kb] >= thr, 0.0, -jnp.inf)
        for g in range(groups):
            rows = pl.ds(g * ag * tq, ag * tq)
            s = lax.dot_general(q_ref[rows, :], k, NT_DIMS, preferred_element_type=F32)
            s = (s.reshape(ag, tq, tk) + drop[None]).reshape(ag * tq, tk)
            _attend(s, v, acc_ref.at[rows], m_ref.at[rows], track_max)
        return carry

    lax.fori_loop(0, nkb, attn_chunk, 0)
    out = _normalised(acc_ref[...])
    for h in range(nh):
        o_ref[:, h * HEAD_DIM:(h + 1) * HEAD_DIM] = out[h * tq:(h + 1) * tq, :].astype(o_ref.dtype)


GROUP_ROWS = 1024
COUNT_ROWS = 64


def _dsa_attention(iq, ik, iw, dq, dk, dv, *, topk, track_max, tq=256, tk=512):
    S, W = dq.shape
    nh = W // HEAD_DIM
    tq, tk = min(tq, S), min(tk, S)
    nkb = S // tk
    full = lambda shape: pl.BlockSpec(shape, lambda i: (0,) * len(shape))
    return pl.pallas_call(
        functools.partial(_dsa_body, tq=tq, tk=tk, topk=topk, nh=nh, track_max=track_max),
        grid=(S // tq,),
        in_specs=[
            pl.BlockSpec((IDX_HEADS, tq, IDX_DIM), lambda i: (0, i, 0)),
            full((S, IDX_DIM)),
            pl.BlockSpec((tq, IDX_HEADS), lambda i: (i, 0)),
            pl.BlockSpec((tq, W), lambda i: (i, 0)),
            full((S, WIDE)),
            full((S, WIDE)),
        ],
        out_specs=pl.BlockSpec((tq, W), lambda i: (i, 0)),
        out_shape=jax.ShapeDtypeStruct((S, W), BF16),
        scratch_shapes=[
            pltpu.VMEM((nkb, tq, tk), I32),
            pltpu.VMEM((nkb, tq, tk), BF16),
            pltpu.VMEM((IDX_HEADS * tq, 1), F32),
            pltpu.VMEM((nh * tq, WIDE), BF16),
            pltpu.VMEM((nh * tq, WIDE), F32),
            pltpu.VMEM((nh * tq, LANES), F32),
        ],
        compiler_params=_params(58, "arbitrary"),
        name="dsa_attention",
    )(iq, ik, iw, dq, dk, dv)


ROUTE_GROUP = 8


NOT_RANKED = float(PEER_NKEYS - 1)


def _top_rows(s, n, with_rank=False):
    rows = []
    rank = jnp.full(s.shape, NOT_RANKED, F32)
    for k in range(n):
        m = jnp.max(s, axis=0, keepdims=True)
        rows.append(m)
        hit = s == m
        if with_rank:
            rank = jnp.where(hit, float(k), rank)
        s = jnp.where(hit, -jnp.inf, s)
    return (rows, rank) if with_rank else rows


def _route_body(qt_ref, kk_ref, n1_ref, e1_ref, r2_ref, e2_ref):
    ts = qt_ref.shape[1]
    ng = PEER_NKEYS // ROUTE_GROUP
    for h in range(PEER_HEADS):
        qh = qt_ref[h * LANES:(h + 1) * LANES, :]
        sc = jnp.dot(kk_ref[h], qh, preferred_element_type=F32, precision=lax.Precision.HIGHEST)
        s1, s2 = sc[:PEER_NKEYS], sc[PEER_NKEYS:]
        v1 = _top_rows(s1, PEER_TOPK)
        v2, rank2 = _top_rows(s2, PEER_TOPK, with_rank=True)
        v2 = jnp.concatenate(v2, axis=0)
        cand = [v1[a] + v2[:-(-(PEER_TOPK // (a + 1)) // 8) * 8] for a in range(PEER_TOPK)]
        best = _top_rows(jnp.concatenate(cand, axis=0), PEER_TOPK)
        tau = best[PEER_TOPK - 1]
        z = sum(jnp.exp(b - best[0]) for b in best)
        n1 = jnp.zeros((PEER_NKEYS, ts), F32)
        for a in range(PEER_TOPK):
            n_a = jnp.sum(jnp.where(cand[a] >= tau, 1.0, 0.0), axis=0, keepdims=True)
            n1 = jnp.where(s1 == v1[a], n_a, n1)
        n1_ref[:, h] = n1.reshape(ng, ROUTE_GROUP, ts)
        e1_ref[:, h] = (jnp.exp(s1 - v1[0]) / z).reshape(ng, ROUTE_GROUP, ts)
        r2_ref[h] = rank2.astype(BF16)
        e2_ref[h] = jnp.exp(s2 - v2[0:1]).astype(BF16)


def _peer_route(qt, kk, ts=256):
    NQ, S = qt.shape
    ts = min(ts, S)
    ng = PEER_NKEYS // ROUTE_GROUP
    grouped = jax.ShapeDtypeStruct((ng, PEER_HEADS, ROUTE_GROUP, S), F32)
    whole = jax.ShapeDtypeStruct((PEER_HEADS, PEER_NKEYS, S), BF16)
    gspec = pl.BlockSpec((ng, PEER_HEADS, ROUTE_GROUP, ts), lambda i: (0, 0, 0, i))
    wspec = pl.BlockSpec((PEER_HEADS, PEER_NKEYS, ts), lambda i: (0, 0, i))
    return pl.pallas_call(
        _route_body,
        grid=(S // ts,),
        in_specs=[pl.BlockSpec((NQ, ts), lambda i: (0, i)),
                  pl.BlockSpec((PEER_HEADS, 2 * PEER_NKEYS, LANES), lambda i: (0, 0, 0))],
        out_specs=(gspec, gspec, wspec, wspec),
        out_shape=(grouped, grouped, whole, whole),
        compiler_params=_params(40, "arbitrary"),
        name="peer_route",
    )(qt, kk)


def _experts_body(x_ref, u_ref, v_ref, n1_ref, e1_ref, r2_ref, e2_ref, res_ref, o_ref, *, nsub):
    eb = pl.program_id(1)

    @pl.when(eb == 0)
    def _():
        o_ref[...] = jnp.zeros_like(o_ref)

    rr = res_ref.shape[0]
    r0 = pl.multiple_of(eb * rr, rr)
    o_ref[pl.ds(r0, rr), :] += res_ref[...]

    first = (eb * nsub) % ROUTE_GROUP
    tb = x_ref.shape[0]
    ht = lax.dot_general(u_ref[...], x_ref[...], NT_DIMS, preferred_element_type=F32)
    parts = []
    reps = PEER_NKEYS // BF16_ROWS

    def row_tile(ref, h, row):
        packed = jnp.broadcast_to(ref[h, pl.ds(row, 1), :], (BF16_ROWS, tb)).astype(BF16)
        return jnp.tile(packed, (reps, 1))

    for j in range(nsub):
        gate = jnp.zeros((PEER_NKEYS, tb), BF16)
        for h in range(PEER_HEADS):
            routed = r2_ref[h] < row_tile(n1_ref, h, first + j)
            gate = gate + jnp.where(routed, e2_ref[h] * row_tile(e1_ref, h, first + j), 0.0)
        hj = ht[j * PEER_NKEYS:(j + 1) * PEER_NKEYS, :]
        act = 0.5 * hj * (1.0 + lax.erf(hj * (2.0 ** -0.5)))
        parts.append(gate.astype(F32) * act)
    a = jnp.concatenate(parts, axis=0).T.astype(BF16)
    o_ref[...] += jnp.dot(a, v_ref[...], preferred_element_type=F32)


BF16_ROWS = 16


def _peer_experts(h2, u, v, n1, e1, r2, e2, res, tb=512, te=1024):
    S, D = h2.shape
    E = u.shape[0]
    tb = min(tb, S)
    nsub = te // PEER_NKEYS
    ne = E // te
    rr = tb // ne
    assert ROUTE_GROUP % nsub == 0 and tb % ne == 0 and rr % 8 == 0
    grouped = pl.BlockSpec((None, PEER_HEADS, ROUTE_GROUP, tb), lambda i, j: ((j * nsub) // ROUTE_GROUP, 0, 0, i))
    once = pl.Buffered(1)
    whole = pl.BlockSpec((PEER_HEADS, PEER_NKEYS, tb), lambda i, j: (0, 0, i), pipeline_mode=once)
    return pl.pallas_call(
        functools.partial(_experts_body, nsub=nsub),
        grid=(S // tb, ne),
        in_specs=[
            pl.BlockSpec((tb, D), lambda i, j: (i, 0), pipeline_mode=once),
            pl.BlockSpec((te, D), lambda i, j: (j, 0)),
            pl.BlockSpec((te, D), lambda i, j: (j, 0)),
            grouped, grouped, whole, whole,
            pl.BlockSpec((rr, D), lambda i, j: (i * ne + j, 0)),
        ],
        out_specs=pl.BlockSpec((tb, D), lambda i, j: (i, 0), pipeline_mode=once),
        out_shape=jax.ShapeDtypeStruct((S, D), F32),
        compiler_params=_params(60, "arbitrary", "arbitrary"),
        name="peer_experts",
    )(h2, u, v, n1, e1, r2, e2, res)


def _relayout_w_tail(w_in, D):
    fh, W, off, total = _tail_layout(D)
    sizes = (fh, W, HEAD_DIM, HEAD_DIM, IDX_HEADS * IDX_DIM, IDX_DIM, IDX_HEADS)
    pts = (3 * W + np.cumsum(sizes)[:-1]).tolist()
    _, ff, dq, dk, dv, iq, ik, iw = jnp.split(w_in, [3 * W] + pts, axis=1)
    zeros = lambda n: jnp.zeros((D, n), w_in.dtype)
    misc = jnp.concatenate([ik, iw, zeros(LANES - IDX_DIM - IDX_HEADS)], axis=1)
    ffp = jnp.concatenate([ff, zeros(LANES - fh)], axis=1)
    cols = [dq, iq, dk, dv, misc, ffp]
    used = sum(c.shape[1] for c in cols)
    if total > used:
        cols.append(zeros(total - used))
    return jnp.concatenate(cols, axis=1).astype(BF16)


def _layer(x, positions, ln1_g, w_in, fox_forget_b, fox_qn_g, fox_kn_g, dsa_qn_g, dsa_kn_g,
           w_o, ln2_g, peer_wq, peer_keys1, peer_keys2, peer_u, peer_v):
    S, D = x.shape
    W = D // 2
    topk = min(DSA_TOPK_MAX, S // 4)

    h1 = _rmsnorm(x, ln1_g, BF16)
    proj_a = _matmul(h1, w_in[:, :3 * W].astype(BF16), name="in_proj_fox")
    proj_b = _matmul(h1, _relayout_w_tail(w_in, D), name="in_proj_dsa")
    fox_safe, fox_half = _logit_span(fox_qn_g, fox_kn_g)
    dsa_safe, dsa_half = _logit_span(dsa_qn_g, dsa_kn_g)
    fq, fk, fv, dq, dk, dv, iq, ik, iw = _post(proj_a, proj_b, positions, fox_forget_b, fox_qn_g, fox_kn_g,
                                               dsa_qn_g, dsa_kn_g, fox_half, dsa_half, D)
    fox_out = lax.cond(fox_safe, functools.partial(_fox_attention, track_max=False),
                       functools.partial(_fox_attention, track_max=True), fq, fk, fv)
    dsa_out = lax.cond(dsa_safe, functools.partial(_dsa_attention, topk=topk, track_max=False),
                       functools.partial(_dsa_attention, topk=topk, track_max=True), iq, ik, iw, dq, dk, dv)
    x1 = _out_proj(fox_out, dsa_out, w_o, x)

    h2 = _rmsnorm(x1, ln2_g, BF16)
    qt = _matmul(peer_wq.T.astype(BF16), h2, nt=True, name="peer_query")
    zk = jnp.zeros_like(peer_keys1)
    kk = jnp.concatenate([jnp.concatenate([peer_keys1, zk], axis=2),
                          jnp.concatenate([zk, peer_keys2], axis=2)], axis=1).astype(F32)
    n1, e1, r2, e2 = _peer_route(qt, kk)
    return _peer_experts(h2, peer_u.astype(BF16), peer_v.astype(BF16), n1, e1, r2, e2, x1)


def kernel(x, positions, ln1_g, w_in, fox_forget_b, fox_qn_g, fox_kn_g, dsa_qn_g, dsa_kn_g, w_o, ln2_g, peer_wq,
           peer_keys1, peer_keys2, peer_u, peer_v):
    B = x.shape[0]
    depth = w_in.shape[0]
    outs = []
    for b in range(B):
        xb = x[b]
        for l in range(depth):
            xb = _layer(xb, positions[b], ln1_g[l], w_in[l], fox_forget_b[l], fox_qn_g[l], fox_kn_g[l],
                        dsa_qn_g[l], dsa_kn_g[l], w_o[l], ln2_g[l], peer_wq[l], peer_keys1[l], peer_keys2[l],
                        peer_u[l], peer_v[l])
        outs.append(xb)
    return outs[0][None] if B == 1 else jnp.stack(outs, axis=0)
```

```python
import functools

import jax
import jax.numpy as jnp
from jax import lax
from jax.experimental import pallas as pl
from jax.experimental.pallas import tpu as pltpu

F32 = jnp.float32
BF16 = jnp.bfloat16
I32 = jnp.int32

HEAD_DIM = 128
IDX_HEADS = 16
IDX_DIM = 64
DSA_TOPK_MAX = 256
ROPE_THETA = 10000.0
NORM_EPS = 1e-6
PEER_HEADS = 8
PEER_NKEYS = 128
PEER_SUBDIM = 64
PEER_TOPK = 16
LOG2E = 1.4426950408889634
LANES = 128
WIDE = 2 * HEAD_DIM
QSCALE = (HEAD_DIM ** -0.5) * LOG2E

LOGIT_BOUND_PER_GAIN = (HEAD_DIM ** 0.5) * LOG2E
SAFE_LOGIT_SPAN = 60.0

NEG_INF_KEY = -2139095041
INT32_MIN = -2147483648

NT_DIMS = (((1,), (1,)), ((), ()))
NN_DIMS = (((1,), (0,)), ((), ()))


def _params(vmem_mb, *sem):
    return pltpu.CompilerParams(dimension_semantics=sem, vmem_limit_bytes=vmem_mb * 1024 * 1024)


def _as_bf16(x):
    return x if x.dtype == BF16 else x.astype(BF16)


def _rmsnorm_body(x_ref, g_ref, o_ref):
    x = x_ref[...].astype(F32)
    y = x * lax.rsqrt(jnp.mean(x * x, axis=-1, keepdims=True) + NORM_EPS)
    o_ref[...] = (y * g_ref[...]).astype(o_ref.dtype)


def _rmsnorm(x2d, g, out_dtype, tm=256):
    S, D = x2d.shape
    tm = min(tm, S)
    return pl.pallas_call(
        _rmsnorm_body,
        grid=(S // tm,),
        in_specs=[pl.BlockSpec((tm, D), lambda i: (i, 0)), pl.BlockSpec((1, D), lambda i: (0, 0))],
        out_specs=pl.BlockSpec((tm, D), lambda i: (i, 0)),
        out_shape=jax.ShapeDtypeStruct((S, D), out_dtype),
        compiler_params=_params(40, "arbitrary"),
        name="rmsnorm",
    )(x2d, g.reshape(1, D).astype(F32))


def _matmul_body(a_ref, b_ref, o_ref, *, nt):
    acc = lax.dot_general(a_ref[...], _as_bf16(b_ref[...]), NT_DIMS if nt else NN_DIMS, preferred_element_type=F32)
    o_ref[...] = acc.astype(o_ref.dtype)


def _matmul(a, b, *, nt=False, n_cols=None, out_dtype=F32, tm=1024, tn=512, name="matmul"):
    M, K = a.shape
    N = n_cols if n_cols is not None else (b.shape[0] if nt else b.shape[1])
    tm = min(tm, M)
    tn = next(t for t in range(min(tn, N), 0, -LANES) if N % t == 0)
    assert M % tm == 0 and tn % LANES == 0
    return pl.pallas_call(
        functools.partial(_matmul_body, nt=nt),
        grid=(M // tm, N // tn),
        in_specs=[
            pl.BlockSpec((tm, K), lambda i, j: (i, 0)),
            pl.BlockSpec((tn, K), lambda i, j: (j, 0)) if nt else pl.BlockSpec((K, tn), lambda i, j: (0, j)),
        ],
        out_specs=pl.BlockSpec((tm, tn), lambda i, j: (i, j)),
        out_shape=jax.ShapeDtypeStruct((M, N), out_dtype),
        compiler_params=_params(48, "arbitrary", "arbitrary"),
        name=name,
    )(a, b)


def _out_proj_body(a1_ref, a2_ref, b1_ref, b2_ref, r_ref, o_ref):
    acc = jnp.dot(a1_ref[...], _as_bf16(b1_ref[...]), preferred_element_type=F32)
    acc = acc + jnp.dot(a2_ref[...], _as_bf16(b2_ref[...]), preferred_element_type=F32)
    o_ref[...] = acc + r_ref[...]


def _out_proj(a1, a2, w, res, tm=1024, tn=512):
    M, K1 = a1.shape
    N = w.shape[1]
    assert a2.shape == a1.shape and w.shape[0] == 2 * K1
    tm, tn = min(tm, M), min(tn, N)
    lhs = pl.BlockSpec((tm, K1), lambda i, j: (i, 0))
    return pl.pallas_call(
        _out_proj_body,
        grid=(M // tm, N // tn),
        in_specs=[lhs, lhs, pl.BlockSpec((K1, tn), lambda i, j: (0, j)), pl.BlockSpec((K1, tn), lambda i, j: (1, j)),
                  pl.BlockSpec((tm, tn), lambda i, j: (i, j))],
        out_specs=pl.BlockSpec((tm, tn), lambda i, j: (i, j)),
        out_shape=jax.ShapeDtypeStruct((M, N), F32),
        compiler_params=_params(48, "arbitrary", "arbitrary"),
        name="out_proj",
    )(a1, a2, w, w, res)


def _tail_layout(D):
    fh = D // (2 * HEAD_DIM)
    W = fh * HEAD_DIM
    off = {}
    pos = 0
    for name, width in (("ff", fh), ("dq", W), ("dk", HEAD_DIM), ("dv", HEAD_DIM),
                        ("iq", IDX_HEADS * IDX_DIM), ("ik", IDX_DIM), ("iw", IDX_HEADS)):
        off[name] = pos
        pos += width
    total = -(-(off["ik"] + LANES) // 512) * 512
    return fh, W, off, total


def _post_body(pa_ref, pb_ref, pos_ref, fb_ref, fqg_ref, fkg_ref, dqg_ref, dkg_ref, inv1_ref, sgn1_ref, inv2_ref,
               sgn2_ref, hbf_ref, hbd_ref,
               fq_ref, fk_ref, fv_ref, dq_ref, dk_ref, dv_ref, iq_ref, ik_ref, iw_ref, carry_ref,
               *, fh, off, tm):
    @pl.when(pl.program_id(0) == 0)
    def _():
        carry_ref[...] = jnp.zeros_like(carry_ref)

    W = fh * HEAD_DIM
    pos = pos_ref[...].astype(F32)
    ang1 = pos * inv1_ref[...]
    cos1, sin1 = jnp.cos(ang1), jnp.sin(ang1) * sgn1_ref[...]
    ang2 = pos * inv2_ref[...]
    cos2, sin2 = jnp.cos(ang2), jnp.sin(ang2) * sgn2_ref[...]
    lane = lax.broadcasted_iota(I32, (tm, LANES), 1)
    lo_half = (lane & (IDX_DIM - 1)) < (IDX_DIM // 2)
    one_hot0 = jnp.where(lane == 0, 1.0, 0.0)

    def head(seg, h):
        s = seg * W + h * LANES
        return pa_ref[:, s:s + LANES]

    def tail(name, j=0):
        s = off[name] + j * LANES
        return pb_ref[:, s:s + LANES]

    def norm(xh, g_ref):
        return xh * lax.rsqrt(jnp.mean(xh * xh, axis=-1, keepdims=True) + NORM_EPS) * g_ref[...]

    def rope_full(xh):
        return xh * cos1 + pltpu.roll(xh, HEAD_DIM // 2, 1) * sin1

    def rope_idx(xh):
        r = jnp.where(lo_half, pltpu.roll(xh, LANES - IDX_DIM // 2, 1), pltpu.roll(xh, IDX_DIM // 2, 1))
        return xh * cos2 + r * sin2

    z = tail("ff") + fb_ref[...]
    c = jnp.minimum(z, 0.0) - jnp.log1p(jnp.exp(-jnp.abs(z)))
    row = lax.broadcasted_iota(I32, (tm, LANES), 0)
    s = 1
    while s < tm:
        c = c + jnp.where(row >= s, pltpu.roll(c, s, 0), 0.0)
        s *= 2
    c = c + carry_ref[...]
    carry_ref[...] = c[tm - 1:tm, :]
    nc = (-LOG2E) * c

    for h in range(fh):
        hs = slice(h * WIDE, h * WIDE + HEAD_DIM)
        xs = slice(h * WIDE + HEAD_DIM, (h + 1) * WIDE)
        fq_ref[:, hs] = (norm(head(0, h), fqg_ref) * QSCALE).astype(BF16)
        fk_ref[:, hs] = norm(head(1, h), fkg_ref).astype(BF16)
        fv_ref[:, hs] = head(2, h).astype(BF16)
        b = jnp.broadcast_to(nc[:, h:h + 1], (tm, LANES))
        hi = b.astype(BF16).astype(F32)
        mid = (b - hi).astype(BF16).astype(F32)
        lo = b - hi - mid
        k_aug = jnp.where(lane == 0, hi, jnp.where(lane == 1, mid, jnp.where(lane == 2, lo,
                                                                             jnp.where(lane == 3, 1.0, 0.0))))
        q_aug = jnp.where(lane < 3, 1.0, jnp.where(lane == 3, -b - hbf_ref[...], 0.0))
        fk_ref[:, xs] = k_aug.astype(BF16)
        fq_ref[:, xs] = q_aug.astype(BF16)
        fv_ref[:, xs] = one_hot0.astype(BF16)
        dq_ref[:, h * LANES:(h + 1) * LANES] = (rope_full(norm(tail("dq", h), dqg_ref)) * QSCALE).astype(BF16)
    dk_ref[:, :HEAD_DIM] = rope_full(norm(tail("dk"), dkg_ref)).astype(BF16)
    dk_ref[:, HEAD_DIM:] = (one_hot0 * (-hbd_ref[...])).astype(BF16)
    dv_ref[:, :HEAD_DIM] = tail("dv").astype(BF16)
    dv_ref[:, HEAD_DIM:] = one_hot0.astype(BF16)

    for j in range(IDX_HEADS // 2):
        blk = rope_idx(tail("iq", j)) * (IDX_DIM ** -0.5)
        iq_ref[2 * j] = blk[:, :IDX_DIM].astype(BF16)
        iq_ref[2 * j + 1] = blk[:, IDX_DIM:].astype(BF16)
    ik_ref[...] = rope_idx(tail("ik"))[:, :IDX_DIM].astype(BF16)
    iw_ref[...] = pb_ref[:, off["iw"]:off["iw"] + IDX_HEADS] * (IDX_HEADS ** -0.5)


def _post(proj_a, proj_b, positions, fb, fqg, fkg, dqg, dkg, half_span_f, half_span_d, D, tm=128):
    S, NB = proj_b.shape
    fh, W, off, total = _tail_layout(D)
    assert total == NB and proj_a.shape == (S, 3 * W)
    tm = min(tm, S)

    def pad_lanes(v):
        return jnp.zeros((1, LANES), F32).at[0, :v.shape[0]].set(v.astype(F32))

    half = HEAD_DIM // 2
    inv_full = ROPE_THETA ** (-jnp.arange(0, HEAD_DIM, 2, dtype=F32) / HEAD_DIM)
    inv1 = jnp.concatenate([inv_full, inv_full]).reshape(1, LANES)
    sgn1 = jnp.concatenate([-jnp.ones(half, F32), jnp.ones(half, F32)]).reshape(1, LANES)
    inv_idx = ROPE_THETA ** (-jnp.arange(0, IDX_DIM, 2, dtype=F32) / IDX_DIM)
    inv2 = jnp.tile(inv_idx, LANES // (IDX_DIM // 2)).reshape(1, LANES)
    q = IDX_DIM // 2
    sgn2 = jnp.tile(jnp.concatenate([-jnp.ones(q, F32), jnp.ones(q, F32)]), LANES // IDX_DIM).reshape(1, LANES)

    row = lambda w: pl.BlockSpec((tm, w), lambda i: (i, 0))
    const = pl.BlockSpec((1, LANES), lambda i: (0, 0))
    wide = jax.ShapeDtypeStruct((S, fh * WIDE), BF16)
    out_shape = (
        wide, wide, wide, jax.ShapeDtypeStruct((S, W), BF16), jax.ShapeDtypeStruct((S, WIDE), BF16),
        jax.ShapeDtypeStruct((S, WIDE), BF16), jax.ShapeDtypeStruct((IDX_HEADS, S, IDX_DIM), BF16),
        jax.ShapeDtypeStruct((S, IDX_DIM), BF16), jax.ShapeDtypeStruct((S, IDX_HEADS), F32),
    )
    out_specs = (
        row(fh * WIDE), row(fh * WIDE), row(fh * WIDE), row(W), row(WIDE), row(WIDE),
        pl.BlockSpec((IDX_HEADS, tm, IDX_DIM), lambda i: (0, i, 0)), row(IDX_DIM), row(IDX_HEADS),
    )
    return pl.pallas_call(
        functools.partial(_post_body, fh=fh, off=off, tm=tm),
        grid=(S // tm,),
        in_specs=[row(3 * W), row(NB), row(1)] + [const] * 11,
        out_specs=out_specs,
        out_shape=out_shape,
        scratch_shapes=[pltpu.VMEM((1, LANES), F32)],
        compiler_params=_params(40, "arbitrary"),
        name="post",
    )(proj_a, proj_b, positions.reshape(S, 1).astype(I32), pad_lanes(fb), fqg.reshape(1, LANES).astype(F32),
      fkg.reshape(1, LANES).astype(F32), dqg.reshape(1, LANES).astype(F32), dkg.reshape(1, LANES).astype(F32),
      inv1, sgn1, inv2, sgn2, jnp.full((1, LANES), half_span_f, F32), jnp.full((1, LANES), half_span_d, F32))


def _attend(s, v, acc_ref, m_ref, track_max):
    if not track_max:
        acc_ref[...] += jnp.dot(jnp.exp2(s).astype(BF16), v, preferred_element_type=F32)
        return
    m_prev = m_ref[...]
    m_new = jnp.maximum(m_prev, jnp.max(s, axis=-1, keepdims=True))
    m_safe = jnp.where(m_new == -jnp.inf, 0.0, m_new)
    alpha = jnp.exp2(m_prev - m_safe)
    p = jnp.exp2(s - jnp.tile(m_safe, (1, s.shape[1] // LANES)))
    acc_ref[...] = jnp.tile(alpha, (1, WIDE // LANES)) * acc_ref[...] + jnp.dot(p.astype(BF16), v,
                                                                                 preferred_element_type=F32)
    m_ref[...] = m_new


def _normalised(acc):
    return acc[:, :HEAD_DIM] / acc[:, HEAD_DIM:HEAD_DIM + 1]


def _logit_span(gq, gk):
    half = LOGIT_BOUND_PER_GAIN * jnp.max(jnp.abs(gq.astype(F32))) * jnp.max(jnp.abs(gk.astype(F32)))
    return 2.0 * half <= SAFE_LOGIT_SPAN, half


def _fox_body(q_ref, k_ref, v_ref, o_ref, acc_ref, m_ref, *, tb, hp, track_max):
    qi = pl.program_id(1)
    acc_ref[...] = jnp.zeros_like(acc_ref)
    if track_max:
        m_ref[...] = jnp.full_like(m_ref, -jnp.inf)

    def step(ki, masked):
        ks = pl.multiple_of(ki * tb, tb)
        for h in range(hp):
            ws = slice(h * WIDE, (h + 1) * WIDE)
            s = lax.dot_general(q_ref[:, ws], k_ref[pl.ds(ks, tb), ws], NT_DIMS, preferred_element_type=F32)
            if masked:
                r = lax.broadcasted_iota(I32, (tb, tb), 0)
                c = lax.broadcasted_iota(I32, (tb, tb), 1)
                s = jnp.where(c <= r, s, -jnp.inf)
            _attend(s, v_ref[pl.ds(ks, tb), ws], acc_ref.at[h], m_ref.at[h], track_max)

    def loop_body(ki, carry):
        step(ki, False)
        return carry

    lax.fori_loop(0, qi, loop_body, 0)
    step(qi, True)
    for h in range(hp):
        o_ref[:, h * HEAD_DIM:(h + 1) * HEAD_DIM] = _normalised(acc_ref[h]).astype(o_ref.dtype)


def _fox_attention(fq, fk, fv, *, track_max, tb=512, hp=4):
    S = fq.shape[0]
    fh = fq.shape[1] // WIDE
    tb = min(tb, S)
    hp = min(hp, fh)
    once = pl.Buffered(1)
    return pl.pallas_call(
        functools.partial(_fox_body, tb=tb, hp=hp, track_max=track_max),
        grid=(fh // hp, S // tb),
        in_specs=[
            pl.BlockSpec((tb, hp * WIDE), lambda h, i: (i, h)),
            pl.BlockSpec((S, hp * WIDE), lambda h, i: (0, h), pipeline_mode=once),
            pl.BlockSpec((S, hp * WIDE), lambda h, i: (0, h), pipeline_mode=once),
        ],
        out_specs=pl.BlockSpec((tb, hp * HEAD_DIM), lambda h, i: (i, h)),
        out_shape=jax.ShapeDtypeStruct((S, fh * HEAD_DIM), BF16),
        scratch_shapes=[pltpu.VMEM((hp, tb, WIDE), F32), pltpu.VMEM((hp, tb, LANES), F32)],
        compiler_params=_params(56, "arbitrary", "arbitrary"),
        name="fox_attention",
    )(fq, fk, fv)


def _sortable_key(x):
    b = lax.bitcast_convert_type(x, I32)
    return b ^ ((b >> 31) & 0x7FFFFFFF)


def _dsa_body(iq_ref, ik_ref, iw_ref, dq_ref, dk_ref, dv_ref, o_ref,
              keys_ref, w_ref, q_ref, acc_ref, m_ref, *, tq, tk, topk, nh, track_max):
    qi = pl.program_id(0)
    groups = max([1] + [g for g in (2, 4, 8) if nh % g == 0 and IDX_HEADS % g == 0 and g * GROUP_ROWS <= nh * tq])
    ig, ag = IDX_HEADS // groups, nh // groups
    nkb = (qi * tq) // tk + 1
    row_g = qi * tq + lax.broadcasted_iota(I32, (tq, tk), 0)
    col_l = lax.broadcasted_iota(I32, (tq, tk), 1)

    for h in range(IDX_HEADS):
        w_ref[h * tq:(h + 1) * tq, :] = iw_ref[:, h:h + 1]

    def score_chunk(kb, carry):
        ks = pl.multiple_of(kb * tk, tk)
        ikb = ik_ref[pl.ds(ks, tk), :]
        sc = None
        for g in range(groups):
            rows = slice(g * ig * tq, (g + 1) * ig * tq)
            iq_g = iq_ref[g * ig:(g + 1) * ig].reshape(ig * tq, IDX_DIM)
            logits = lax.dot_general(iq_g, ikb, NT_DIMS, preferred_element_type=F32)
            part = jnp.sum((jnp.maximum(logits, 0.0) * w_ref[rows, :]).reshape(ig, tq, tk), axis=0)
            sc = part if sc is None else sc + part
        sc = jnp.where(ks + col_l <= row_g, sc, -jnp.inf)
        keys_ref[kb] = _sortable_key(sc)
        return carry

    lax.fori_loop(0, nkb, score_chunk, 0)

    def bit_step(i, thr):
        cand = thr + lax.shift_left(jnp.int32(1), 31 - i)

        cand_rows = [jnp.broadcast_to(cand[r:r + COUNT_ROWS], (COUNT_ROWS, LANES)) for r in range(0, tq, COUNT_ROWS)]

        def count_chunk(kb, cnts):
            out = []
            for n, r in enumerate(range(0, tq, COUNT_ROWS)):
                c = cnts[n]
                for j in range(tk // LANES):
                    k = keys_ref[kb, r:r + COUNT_ROWS, j * LANES:(j + 1) * LANES]
                    c = c + jnp.where(k >= cand_rows[n], 1.0, 0.0)
                out.append(c)
            return tuple(out)

        zero = jnp.zeros((COUNT_ROWS, LANES), F32)
        cnts = lax.fori_loop(0, nkb, count_chunk, (zero,) * (tq // COUNT_ROWS))
        cnt = jnp.sum(jnp.concatenate(cnts, axis=0), axis=-1, keepdims=True)
        return jnp.where(cnt >= float(topk), cand, thr)

    thr = lax.fori_loop(0, 32, bit_step, jnp.full((tq, 1), INT32_MIN, I32))
    thr = jnp.maximum(thr, NEG_INF_KEY + 1)

    lane = lax.broadcasted_iota(I32, (nh * tq, LANES), 1)
    q_ref[:, HEAD_DIM:] = jnp.where(lane == 0, 1.0, 0.0).astype(BF16)
    for h in range(nh):
        q_ref[h * tq:(h + 1) * tq, :HEAD_DIM] = dq_ref[:, h * HEAD_DIM:(h + 1) * HEAD_DIM]
    acc_ref[...] = jnp.zeros_like(acc_ref)
    if track_max:
        m_ref[...] = jnp.full_like(m_ref, -jnp.inf)

    def attn_chunk(kb, carry):
        ks = pl.multiple_of(kb * tk, tk)
        k = dk_ref[pl.ds(ks, tk), :]
        v = dv_ref[pl.ds(ks, tk), :]
        drop = jnp.where(keys_ref[kb] >= thr, 0.0, -jnp.inf)
        for g in range(groups):
            rows = pl.ds(g * ag * tq, ag * tq)
            s = lax.dot_general(q_ref[rows, :], k, NT_DIMS, preferred_element_type=F32)
            s = (s.reshape(ag, tq, tk) + drop[None]).reshape(ag * tq, tk)
            _attend(s, v, acc_ref.at[rows], m_ref.at[rows], track_max)
        return carry

    lax.fori_loop(0, nkb, attn_chunk, 0)
    out = _normalised(acc_ref[...])
    for h in range(nh):
        o_ref[:, h * HEAD_DIM:(h + 1) * HEAD_DIM] = out[h * tq:(h + 1) * tq, :].astype(o_ref.dtype)


GROUP_ROWS = 1024
COUNT_ROWS = 64


def _dsa_attention(iq, ik, iw, dq, dk, dv, *, topk, track_max, tq=256, tk=512):
    S, W = dq.shape
    nh = W // HEAD_DIM
    tq, tk = min(tq, S), min(tk, S)
    nkb = S // tk
    full = lambda shape: pl.BlockSpec(shape, lambda i: (0,) * len(shape))
    return pl.pallas_call(
        functools.partial(_dsa_body, tq=tq, tk=tk, topk=topk, nh=nh, track_max=track_max),
        grid=(S // tq,),
        in_specs=[
            pl.BlockSpec((IDX_HEADS, tq, IDX_DIM), lambda i: (0, i, 0)),
            full((S, IDX_DIM)),
            pl.BlockSpec((tq, IDX_HEADS), lambda i: (i, 0)),
            pl.BlockSpec((tq, W), lambda i: (i, 0)),
            full((S, WIDE)),
            full((S, WIDE)),
        ],
        out_specs=pl.BlockSpec((tq, W), lambda i: (i, 0)),
        out_shape=jax.ShapeDtypeStruct((S, W), BF16),
        scratch_shapes=[
            pltpu.VMEM((nkb, tq, tk), I32),
            pltpu.VMEM((IDX_HEADS * tq, 1), F32),
            pltpu.VMEM((nh * tq, WIDE), BF16),
            pltpu.VMEM((nh * tq, WIDE), F32),
            pltpu.VMEM((nh * tq, LANES), F32),
        ],
        compiler_params=_params(58, "arbitrary"),
        name="dsa_attention",
    )(iq, ik, iw, dq, dk, dv)


ROUTE_GROUP = 8


NOT_RANKED = float(PEER_NKEYS - 1)


def _top_rows(s, n, with_rank=False):
    rows = []
    rank = jnp.full(s.shape, NOT_RANKED, F32)
    for k in range(n):
        m = jnp.max(s, axis=0, keepdims=True)
        rows.append(m)
        hit = s == m
        if with_rank:
            rank = jnp.where(hit, float(k), rank)
        s = jnp.where(hit, -jnp.inf, s)
    return (rows, rank) if with_rank else rows


def _route_body(qt_ref, kk_ref, n1_ref, e1_ref, r2_ref, e2_ref):
    ts = qt_ref.shape[1]
    ng = PEER_NKEYS // ROUTE_GROUP
    for h in range(PEER_HEADS):
        qh = qt_ref[h * LANES:(h + 1) * LANES, :]
        sc = jnp.dot(kk_ref[h], qh, preferred_element_type=F32, precision=lax.Precision.HIGHEST)
        s1, s2 = sc[:PEER_NKEYS], sc[PEER_NKEYS:]
        v1 = _top_rows(s1, PEER_TOPK)
        v2, rank2 = _top_rows(s2, PEER_TOPK, with_rank=True)
        v2 = jnp.concatenate(v2, axis=0)
        cand = [v1[a] + v2[:-(-(PEER_TOPK // (a + 1)) // 8) * 8] for a in range(PEER_TOPK)]
        best = _top_rows(jnp.concatenate(cand, axis=0), PEER_TOPK)
        tau = best[PEER_TOPK - 1]
        z = sum(jnp.exp(b - best[0]) for b in best)
        n1 = jnp.zeros((PEER_NKEYS, ts), F32)
        for a in range(PEER_TOPK):
            n_a = jnp.sum(jnp.where(cand[a] >= tau, 1.0, 0.0), axis=0, keepdims=True)
            n1 = jnp.where(s1 == v1[a], n_a, n1)
        n1_ref[:, h] = n1.reshape(ng, ROUTE_GROUP, ts)
        e1_ref[:, h] = (jnp.exp(s1 - v1[0]) / z).reshape(ng, ROUTE_GROUP, ts)
        r2_ref[h] = rank2.astype(BF16)
        e2_ref[h] = jnp.exp(s2 - v2[0:1]).astype(BF16)


def _peer_route(qt, kk, ts=256):
    NQ, S = qt.shape
    ts = min(ts, S)
    ng = PEER_NKEYS // ROUTE_GROUP
    grouped = jax.ShapeDtypeStruct((ng, PEER_HEADS, ROUTE_GROUP, S), F32)
    whole = jax.ShapeDtypeStruct((PEER_HEADS, PEER_NKEYS, S), BF16)
    gspec = pl.BlockSpec((ng, PEER_HEADS, ROUTE_GROUP, ts), lambda i: (0, 0, 0, i))
    wspec = pl.BlockSpec((PEER_HEADS, PEER_NKEYS, ts), lambda i: (0, 0, i))
    return pl.pallas_call(
        _route_body,
        grid=(S // ts,),
        in_specs=[pl.BlockSpec((NQ, ts), lambda i: (0, i)),
                  pl.BlockSpec((PEER_HEADS, 2 * PEER_NKEYS, LANES), lambda i: (0, 0, 0))],
        out_specs=(gspec, gspec, wspec, wspec),
        out_shape=(grouped, grouped, whole, whole),
        compiler_params=_params(40, "arbitrary"),
        name="peer_route",
    )(qt, kk)


def _experts_body(x_ref, u_ref, v_ref, n1_ref, e1_ref, r2_ref, e2_ref, res_ref, o_ref, *, nsub):
    eb = pl.program_id(1)

    @pl.when(eb == 0)
    def _():
        o_ref[...] = jnp.zeros_like(o_ref)

    rr = res_ref.shape[0]
    r0 = pl.multiple_of(eb * rr, rr)
    o_ref[pl.ds(r0, rr), :] += res_ref[...]

    first = (eb * nsub) % ROUTE_GROUP
    tb = x_ref.shape[0]
    ht = lax.dot_general(u_ref[...], x_ref[...], NT_DIMS, preferred_element_type=F32)
    parts = []
    reps = PEER_NKEYS // BF16_ROWS

    def row_tile(ref, h, row):
        packed = jnp.broadcast_to(ref[h, pl.ds(row, 1), :], (BF16_ROWS, tb)).astype(BF16)
        return jnp.tile(packed, (reps, 1))

    for j in range(nsub):
        gate = jnp.zeros((PEER_NKEYS, tb), BF16)
        for h in range(PEER_HEADS):
            routed = r2_ref[h] < row_tile(n1_ref, h, first + j)
            gate = gate + jnp.where(routed, e2_ref[h] * row_tile(e1_ref, h, first + j), 0.0)
        hj = ht[j * PEER_NKEYS:(j + 1) * PEER_NKEYS, :]
        act = 0.5 * hj * (1.0 + lax.erf(hj * (2.0 ** -0.5)))
        parts.append(gate.astype(F32) * act)
    a = jnp.concatenate(parts, axis=0).T.astype(BF16)
    o_ref[...] += jnp.dot(a, v_ref[...], preferred_element_type=F32)


BF16_ROWS = 16


def _peer_experts(h2, u, v, n1, e1, r2, e2, res, tb=512, te=1024):
    S, D = h2.shape
    E = u.shape[0]
    tb = min(tb, S)
    nsub = te // PEER_NKEYS
    ne = E // te
    rr = tb // ne
    assert ROUTE_GROUP % nsub == 0 and tb % ne == 0 and rr % 8 == 0
    grouped = pl.BlockSpec((None, PEER_HEADS, ROUTE_GROUP, tb), lambda i, j: ((j * nsub) // ROUTE_GROUP, 0, 0, i))
    once = pl.Buffered(1)
    whole = pl.BlockSpec((PEER_HEADS, PEER_NKEYS, tb), lambda i, j: (0, 0, i), pipeline_mode=once)
    return pl.pallas_call(
        functools.partial(_experts_body, nsub=nsub),
        grid=(S // tb, ne),
        in_specs=[
            pl.BlockSpec((tb, D), lambda i, j: (i, 0), pipeline_mode=once),
            pl.BlockSpec((te, D), lambda i, j: (j, 0)),
            pl.BlockSpec((te, D), lambda i, j: (j, 0)),
            grouped, grouped, whole, whole,
            pl.BlockSpec((rr, D), lambda i, j: (i * ne + j, 0)),
        ],
        out_specs=pl.BlockSpec((tb, D), lambda i, j: (i, 0), pipeline_mode=once),
        out_shape=jax.ShapeDtypeStruct((S, D), F32),
        compiler_params=_params(60, "arbitrary", "arbitrary"),
        name="peer_experts",
    )(h2, u, v, n1, e1, r2, e2, res)


def _w_tail(w_in, D):
    fh, W, off, total = _tail_layout(D)
    tail = w_in[:, 3 * W:].astype(BF16)
    return jnp.pad(tail, ((0, 0), (0, total - tail.shape[1])))


def _layer(x, positions, ln1_g, w_in, fox_forget_b, fox_qn_g, fox_kn_g, dsa_qn_g, dsa_kn_g,
           w_o, ln2_g, peer_wq, peer_keys1, peer_keys2, peer_u, peer_v):
    S, D = x.shape
    W = D // 2
    topk = min(DSA_TOPK_MAX, S // 4)

    h1 = _rmsnorm(x, ln1_g, BF16)
    proj_a = _matmul(h1, w_in[:, :3 * W].astype(BF16), name="in_proj_fox")
    proj_b = _matmul(h1, _w_tail(w_in, D), name="in_proj_dsa")
    fox_safe, fox_half = _logit_span(fox_qn_g, fox_kn_g)
    dsa_safe, dsa_half = _logit_span(dsa_qn_g, dsa_kn_g)
    fq, fk, fv, dq, dk, dv, iq, ik, iw = _post(proj_a, proj_b, positions, fox_forget_b, fox_qn_g, fox_kn_g,
                                               dsa_qn_g, dsa_kn_g, fox_half, dsa_half, D)
    fox_out = lax.cond(fox_safe, functools.partial(_fox_attention, track_max=False),
                       functools.partial(_fox_attention, track_max=True), fq, fk, fv)
    dsa_out = lax.cond(dsa_safe, functools.partial(_dsa_attention, topk=topk, track_max=False),
                       functools.partial(_dsa_attention, topk=topk, track_max=True), iq, ik, iw, dq, dk, dv)
    x1 = _out_proj(fox_out, dsa_out, w_o, x)

    h2 = _rmsnorm(x1, ln2_g, BF16)
    qt = _matmul(peer_wq.T.astype(BF16), h2, nt=True, name="peer_query")
    zk = jnp.zeros_like(peer_keys1)
    kk = jnp.concatenate([jnp.concatenate([peer_keys1, zk], axis=2),
                          jnp.concatenate([zk, peer_keys2], axis=2)], axis=1).astype(F32)
    n1, e1, r2, e2 = _peer_route(qt, kk)
    return _peer_experts(h2, peer_u.astype(BF16), peer_v.astype(BF16), n1, e1, r2, e2, x1)


def kernel(x, positions, ln1_g, w_in, fox_forget_b, fox_qn_g, fox_kn_g, dsa_qn_g, dsa_kn_g, w_o, ln2_g, peer_wq,
           peer_keys1, peer_keys2, peer_u, peer_v):
    B = x.shape[0]
    depth = w_in.shape[0]
    outs = []
    for b in range(B):
        xb = x[b]
        for l in range(depth):
            xb = _layer(xb, positions[b], ln1_g[l], w_in[l], fox_forget_b[l], fox_qn_g[l], fox_kn_g[l],
                        dsa_qn_g[l], dsa_kn_g[l], w_o[l], ln2_g[l], peer_wq[l], peer_keys1[l], peer_keys2[l],
                        peer_u[l], peer_v[l])
        outs.append(xb)
    return outs[0][None] if B == 1 else jnp.stack(outs, axis=0)
```

```python
import functools

import numpy as np
import jax
import jax.numpy as jnp
from jax import lax
from jax.experimental import pallas as pl
from jax.experimental.pallas import tpu as pltpu

F32 = jnp.float32
BF16 = jnp.bfloat16
I32 = jnp.int32

HEAD_DIM = 128
IDX_HEADS = 16
IDX_DIM = 64
DSA_TOPK_MAX = 256
ROPE_THETA = 10000.0
NORM_EPS = 1e-6
PEER_HEADS = 8
PEER_NKEYS = 128
PEER_SUBDIM = 64
PEER_TOPK = 16
LOG2E = 1.4426950408889634
LANES = 128
WIDE = 2 * HEAD_DIM
QSCALE = (HEAD_DIM ** -0.5) * LOG2E

LOGIT_BOUND_PER_GAIN = (HEAD_DIM ** 0.5) * LOG2E
SAFE_LOGIT_SPAN = 60.0

NEG_INF_KEY = -2139095041
INT32_MIN = -2147483648

NT_DIMS = (((1,), (1,)), ((), ()))
NN_DIMS = (((1,), (0,)), ((), ()))


def _params(vmem_mb, *sem):
    return pltpu.CompilerParams(dimension_semantics=sem, vmem_limit_bytes=vmem_mb * 1024 * 1024)


def _as_bf16(x):
    return x if x.dtype == BF16 else x.astype(BF16)


def _rmsnorm_body(x_ref, g_ref, o_ref):
    x = x_ref[...].astype(F32)
    y = x * lax.rsqrt(jnp.mean(x * x, axis=-1, keepdims=True) + NORM_EPS)
    o_ref[...] = (y * g_ref[...]).astype(o_ref.dtype)


def _rmsnorm(x2d, g, out_dtype, tm=256):
    S, D = x2d.shape
    tm = min(tm, S)
    return pl.pallas_call(
        _rmsnorm_body,
        grid=(S // tm,),
        in_specs=[pl.BlockSpec((tm, D), lambda i: (i, 0)), pl.BlockSpec((1, D), lambda i: (0, 0))],
        out_specs=pl.BlockSpec((tm, D), lambda i: (i, 0)),
        out_shape=jax.ShapeDtypeStruct((S, D), out_dtype),
        compiler_params=_params(40, "arbitrary"),
        name="rmsnorm",
    )(x2d, g.reshape(1, D).astype(F32))


def _matmul_body(a_ref, b_ref, o_ref, *, nt):
    acc = lax.dot_general(a_ref[...], _as_bf16(b_ref[...]), NT_DIMS if nt else NN_DIMS, preferred_element_type=F32)
    o_ref[...] = acc.astype(o_ref.dtype)


def _matmul(a, b, *, nt=False, n_cols=None, out_dtype=F32, tm=1024, tn=512, name="matmul"):
    M, K = a.shape
    N = n_cols if n_cols is not None else (b.shape[0] if nt else b.shape[1])
    tm = min(tm, M)
    tn = next(t for t in range(min(tn, N), 0, -LANES) if N % t == 0)
    assert M % tm == 0 and tn % LANES == 0
    return pl.pallas_call(
        functools.partial(_matmul_body, nt=nt),
        grid=(M // tm, N // tn),
        in_specs=[
            pl.BlockSpec((tm, K), lambda i, j: (i, 0)),
            pl.BlockSpec((tn, K), lambda i, j: (j, 0)) if nt else pl.BlockSpec((K, tn), lambda i, j: (0, j)),
        ],
        out_specs=pl.BlockSpec((tm, tn), lambda i, j: (i, j)),
        out_shape=jax.ShapeDtypeStruct((M, N), out_dtype),
        compiler_params=_params(48, "arbitrary", "arbitrary"),
        name=name,
    )(a, b)


def _out_proj_body(a1_ref, a2_ref, b1_ref, b2_ref, r_ref, o_ref):
    acc = jnp.dot(a1_ref[...], _as_bf16(b1_ref[...]), preferred_element_type=F32)
    acc = acc + jnp.dot(a2_ref[...], _as_bf16(b2_ref[...]), preferred_element_type=F32)
    o_ref[...] = acc + r_ref[...]


def _out_proj(a1, a2, w, res, tm=1024, tn=512):
    M, K1 = a1.shape
    N = w.shape[1]
    assert a2.shape == a1.shape and w.shape[0] == 2 * K1
    tm, tn = min(tm, M), min(tn, N)
    lhs = pl.BlockSpec((tm, K1), lambda i, j: (i, 0))
    return pl.pallas_call(
        _out_proj_body,
        grid=(M // tm, N // tn),
        in_specs=[lhs, lhs, pl.BlockSpec((K1, tn), lambda i, j: (0, j)), pl.BlockSpec((K1, tn), lambda i, j: (1, j)),
                  pl.BlockSpec((tm, tn), lambda i, j: (i, j))],
        out_specs=pl.BlockSpec((tm, tn), lambda i, j: (i, j)),
        out_shape=jax.ShapeDtypeStruct((M, N), F32),
        compiler_params=_params(48, "arbitrary", "arbitrary"),
        name="out_proj",
    )(a1, a2, w, w, res)


def _tail_layout(D):
    fh = D // (2 * HEAD_DIM)
    W = fh * HEAD_DIM
    off = {}
    pos = 0
    for name, width in (("dq", W), ("iq", IDX_HEADS * IDX_DIM), ("dk", HEAD_DIM), ("dv", HEAD_DIM),
                        ("misc", LANES), ("ff", LANES)):
        off[name] = pos
        pos += width
    total = -(-pos // 512) * 512
    return fh, W, off, total


def _post_body(pa_ref, pb_ref, pos_ref, fb_ref, fqg_ref, fkg_ref, dqg_ref, dkg_ref, inv1_ref, sgn1_ref, inv2_ref,
               sgn2_ref, hbf_ref, hbd_ref,
               fq_ref, fk_ref, fv_ref, dq_ref, dk_ref, dv_ref, iq_ref, ik_ref, iw_ref, carry_ref,
               *, fh, off, tm):
    @pl.when(pl.program_id(0) == 0)
    def _():
        carry_ref[...] = jnp.zeros_like(carry_ref)

    W = fh * HEAD_DIM
    pos = pos_ref[...].astype(F32)
    ang1 = pos * inv1_ref[...]
    cos1, sin1 = jnp.cos(ang1), jnp.sin(ang1) * sgn1_ref[...]
    ang2 = pos * inv2_ref[...]
    cos2, sin2 = jnp.cos(ang2), jnp.sin(ang2) * sgn2_ref[...]
    lane = lax.broadcasted_iota(I32, (tm, LANES), 1)
    lo_half = (lane & (IDX_DIM - 1)) < (IDX_DIM // 2)
    one_hot0 = jnp.where(lane == 0, 1.0, 0.0)

    def head(seg, h):
        s = seg * W + h * LANES
        return pa_ref[:, s:s + LANES]

    def tail(name, j=0):
        s = off[name] + j * LANES
        return pb_ref[:, s:s + LANES]

    def norm(xh, g_ref):
        return xh * lax.rsqrt(jnp.mean(xh * xh, axis=-1, keepdims=True) + NORM_EPS) * g_ref[...]

    def rope_full(xh):
        return xh * cos1 + pltpu.roll(xh, HEAD_DIM // 2, 1) * sin1

    def rope_idx(xh):
        r = jnp.where(lo_half, pltpu.roll(xh, LANES - IDX_DIM // 2, 1), pltpu.roll(xh, IDX_DIM // 2, 1))
        return xh * cos2 + r * sin2

    z = tail("ff") + fb_ref[...]
    c = jnp.minimum(z, 0.0) - jnp.log1p(jnp.exp(-jnp.abs(z)))
    row = lax.broadcasted_iota(I32, (tm, LANES), 0)
    s = 1
    while s < tm:
        c = c + jnp.where(row >= s, pltpu.roll(c, s, 0), 0.0)
        s *= 2
    c = c + carry_ref[...]
    carry_ref[...] = c[tm - 1:tm, :]
    nc = (-LOG2E) * c

    for h in range(fh):
        hs = slice(h * WIDE, h * WIDE + HEAD_DIM)
        xs = slice(h * WIDE + HEAD_DIM, (h + 1) * WIDE)
        fq_ref[:, hs] = (norm(head(0, h), fqg_ref) * QSCALE).astype(BF16)
        fk_ref[:, hs] = norm(head(1, h), fkg_ref).astype(BF16)
        fv_ref[:, hs] = head(2, h).astype(BF16)
        b = jnp.broadcast_to(nc[:, h:h + 1], (tm, LANES))
        hi = b.astype(BF16).astype(F32)
        mid = (b - hi).astype(BF16).astype(F32)
        lo = b - hi - mid
        k_aug = jnp.where(lane == 0, hi, jnp.where(lane == 1, mid, jnp.where(lane == 2, lo,
                                                                             jnp.where(lane == 3, 1.0, 0.0))))
        q_aug = jnp.where(lane < 3, 1.0, jnp.where(lane == 3, -b - hbf_ref[...], 0.0))
        fk_ref[:, xs] = k_aug.astype(BF16)
        fq_ref[:, xs] = q_aug.astype(BF16)
        fv_ref[:, xs] = one_hot0.astype(BF16)
        dq_ref[:, h * LANES:(h + 1) * LANES] = (rope_full(norm(tail("dq", h), dqg_ref)) * QSCALE).astype(BF16)
    dk_ref[:, :HEAD_DIM] = rope_full(norm(tail("dk"), dkg_ref)).astype(BF16)
    dk_ref[:, HEAD_DIM:] = (one_hot0 * (-hbd_ref[...])).astype(BF16)
    dv_ref[:, :HEAD_DIM] = tail("dv").astype(BF16)
    dv_ref[:, HEAD_DIM:] = one_hot0.astype(BF16)

    for j in range(IDX_HEADS // 2):
        blk = rope_idx(tail("iq", j)) * (IDX_DIM ** -0.5)
        iq_ref[2 * j] = blk[:, :IDX_DIM].astype(BF16)
        iq_ref[2 * j + 1] = blk[:, IDX_DIM:].astype(BF16)
    misc = tail("misc")
    ik_ref[...] = rope_idx(misc)[:, :IDX_DIM].astype(BF16)
    iw_ref[...] = misc[:, IDX_DIM:IDX_DIM + IDX_HEADS] * (IDX_HEADS ** -0.5)


def _post(proj_a, proj_b, positions, fb, fqg, fkg, dqg, dkg, half_span_f, half_span_d, D, tm=128):
    S, NB = proj_b.shape
    fh, W, off, total = _tail_layout(D)
    assert total == NB and proj_a.shape == (S, 3 * W)
    tm = min(tm, S)

    def pad_lanes(v):
        return jnp.zeros((1, LANES), F32).at[0, :v.shape[0]].set(v.astype(F32))

    half = HEAD_DIM // 2
    inv_full = ROPE_THETA ** (-jnp.arange(0, HEAD_DIM, 2, dtype=F32) / HEAD_DIM)
    inv1 = jnp.concatenate([inv_full, inv_full]).reshape(1, LANES)
    sgn1 = jnp.concatenate([-jnp.ones(half, F32), jnp.ones(half, F32)]).reshape(1, LANES)
    inv_idx = ROPE_THETA ** (-jnp.arange(0, IDX_DIM, 2, dtype=F32) / IDX_DIM)
    inv2 = jnp.tile(inv_idx, LANES // (IDX_DIM // 2)).reshape(1, LANES)
    q = IDX_DIM // 2
    sgn2 = jnp.tile(jnp.concatenate([-jnp.ones(q, F32), jnp.ones(q, F32)]), LANES // IDX_DIM).reshape(1, LANES)

    row = lambda w: pl.BlockSpec((tm, w), lambda i: (i, 0))
    const = pl.BlockSpec((1, LANES), lambda i: (0, 0))
    wide = jax.ShapeDtypeStruct((S, fh * WIDE), BF16)
    out_shape = (
        wide, wide, wide, jax.ShapeDtypeStruct((S, W), BF16), jax.ShapeDtypeStruct((S, WIDE), BF16),
        jax.ShapeDtypeStruct((S, WIDE), BF16), jax.ShapeDtypeStruct((IDX_HEADS, S, IDX_DIM), BF16),
        jax.ShapeDtypeStruct((S, IDX_DIM), BF16), jax.ShapeDtypeStruct((S, IDX_HEADS), F32),
    )
    out_specs = (
        row(fh * WIDE), row(fh * WIDE), row(fh * WIDE), row(W), row(WIDE), row(WIDE),
        pl.BlockSpec((IDX_HEADS, tm, IDX_DIM), lambda i: (0, i, 0)), row(IDX_DIM), row(IDX_HEADS),
    )
    return pl.pallas_call(
        functools.partial(_post_body, fh=fh, off=off, tm=tm),
        grid=(S // tm,),
        in_specs=[row(3 * W), row(NB), row(1)] + [const] * 11,
        out_specs=out_specs,
        out_shape=out_shape,
        scratch_shapes=[pltpu.VMEM((1, LANES), F32)],
        compiler_params=_params(40, "arbitrary"),
        name="post",
    )(proj_a, proj_b, positions.reshape(S, 1).astype(I32), pad_lanes(fb), fqg.reshape(1, LANES).astype(F32),
      fkg.reshape(1, LANES).astype(F32), dqg.reshape(1, LANES).astype(F32), dkg.reshape(1, LANES).astype(F32),
      inv1, sgn1, inv2, sgn2, jnp.full((1, LANES), half_span_f, F32), jnp.full((1, LANES), half_span_d, F32))


def _attend(s, v, acc_ref, m_ref, track_max):
    if not track_max:
        acc_ref[...] += jnp.dot(jnp.exp2(s).astype(BF16), v, preferred_element_type=F32)
        return
    m_prev = m_ref[...]
    m_new = jnp.maximum(m_prev, jnp.max(s, axis=-1, keepdims=True))
    m_safe = jnp.where(m_new == -jnp.inf, 0.0, m_new)
    alpha = jnp.exp2(m_prev - m_safe)
    p = jnp.exp2(s - jnp.tile(m_safe, (1, s.shape[1] // LANES)))
    acc_ref[...] = jnp.tile(alpha, (1, WIDE // LANES)) * acc_ref[...] + jnp.dot(p.astype(BF16), v,
                                                                                 preferred_element_type=F32)
    m_ref[...] = m_new


def _normalised(acc):
    return acc[:, :HEAD_DIM] / acc[:, HEAD_DIM:HEAD_DIM + 1]


def _logit_span(gq, gk):
    half = LOGIT_BOUND_PER_GAIN * jnp.max(jnp.abs(gq.astype(F32))) * jnp.max(jnp.abs(gk.astype(F32)))
    return 2.0 * half <= SAFE_LOGIT_SPAN, half


def _fox_body(q_ref, k_ref, v_ref, o_ref, acc_ref, m_ref, *, tb, hp, track_max):
    qi = pl.program_id(1)
    acc_ref[...] = jnp.zeros_like(acc_ref)
    if track_max:
        m_ref[...] = jnp.full_like(m_ref, -jnp.inf)

    def step(ki, masked):
        ks = pl.multiple_of(ki * tb, tb)
        for h in range(hp):
            ws = slice(h * WIDE, (h + 1) * WIDE)
            s = lax.dot_general(q_ref[:, ws], k_ref[pl.ds(ks, tb), ws], NT_DIMS, preferred_element_type=F32)
            if masked:
                r = lax.broadcasted_iota(I32, (tb, tb), 0)
                c = lax.broadcasted_iota(I32, (tb, tb), 1)
                s = jnp.where(c <= r, s, -jnp.inf)
            _attend(s, v_ref[pl.ds(ks, tb), ws], acc_ref.at[h], m_ref.at[h], track_max)

    def loop_body(ki, carry):
        step(ki, False)
        return carry

    lax.fori_loop(0, qi, loop_body, 0)
    step(qi, True)
    for h in range(hp):
        o_ref[:, h * HEAD_DIM:(h + 1) * HEAD_DIM] = _normalised(acc_ref[h]).astype(o_ref.dtype)


def _fox_attention(fq, fk, fv, *, track_max, tb=512, hp=4):
    S = fq.shape[0]
    fh = fq.shape[1] // WIDE
    tb = min(tb, S)
    hp = min(hp, fh)
    once = pl.Buffered(1)
    return pl.pallas_call(
        functools.partial(_fox_body, tb=tb, hp=hp, track_max=track_max),
        grid=(fh // hp, S // tb),
        in_specs=[
            pl.BlockSpec((tb, hp * WIDE), lambda h, i: (i, h)),
            pl.BlockSpec((S, hp * WIDE), lambda h, i: (0, h), pipeline_mode=once),
            pl.BlockSpec((S, hp * WIDE), lambda h, i: (0, h), pipeline_mode=once),
        ],
        out_specs=pl.BlockSpec((tb, hp * HEAD_DIM), lambda h, i: (i, h)),
        out_shape=jax.ShapeDtypeStruct((S, fh * HEAD_DIM), BF16),
        scratch_shapes=[pltpu.VMEM((hp, tb, WIDE), F32), pltpu.VMEM((hp, tb, LANES), F32)],
        compiler_params=_params(56, "arbitrary", "arbitrary"),
        name="fox_attention",
    )(fq, fk, fv)


def _sortable_key(x):
    b = lax.bitcast_convert_type(x, I32)
    return b ^ ((b >> 31) & 0x7FFFFFFF)


def _dsa_body(iq_ref, ik_ref, iw_ref, dq_ref, dk_ref, dv_ref, o_ref,
              keys_ref, w_ref, q_ref, acc_ref, m_ref, *, tq, tk, topk, nh, track_max):
    qi = pl.program_id(0)
    groups = max([1] + [g for g in (2, 4, 8) if nh % g == 0 and IDX_HEADS % g == 0 and g * GROUP_ROWS <= nh * tq])
    ig, ag = IDX_HEADS // groups, nh // groups
    nkb = (qi * tq) // tk + 1
    row_g = qi * tq + lax.broadcasted_iota(I32, (tq, tk), 0)
    col_l = lax.broadcasted_iota(I32, (tq, tk), 1)

    for h in range(IDX_HEADS):
        w_ref[h * tq:(h + 1) * tq, :] = iw_ref[:, h:h + 1]

    def score_chunk(kb, carry):
        ks = pl.multiple_of(kb * tk, tk)
        ikb = ik_ref[pl.ds(ks, tk), :]
        sc = None
        for g in range(groups):
            rows = slice(g * ig * tq, (g + 1) * ig * tq)
            iq_g = iq_ref[g * ig:(g + 1) * ig].reshape(ig * tq, IDX_DIM)
            logits = lax.dot_general(iq_g, ikb, NT_DIMS, preferred_element_type=F32)
            part = jnp.sum((jnp.maximum(logits, 0.0) * w_ref[rows, :]).reshape(ig, tq, tk), axis=0)
            sc = part if sc is None else sc + part
        sc = jnp.where(ks + col_l <= row_g, sc, -jnp.inf)
        keys_ref[kb] = _sortable_key(sc)
        return carry

    lax.fori_loop(0, nkb, score_chunk, 0)

    def bit_step(i, thr):
        cand = thr + lax.shift_left(jnp.int32(1), 31 - i)

        cand_rows = [jnp.broadcast_to(cand[r:r + COUNT_ROWS], (COUNT_ROWS, LANES)) for r in range(0, tq, COUNT_ROWS)]

        def count_chunk(kb, cnts):
            out = []
            for n, r in enumerate(range(0, tq, COUNT_ROWS)):
                c = cnts[n]
                for j in range(tk // LANES):
                    k = keys_ref[kb, r:r + COUNT_ROWS, j * LANES:(j + 1) * LANES]
                    c = c + jnp.where(k >= cand_rows[n], 1.0, 0.0)
                out.append(c)
            return tuple(out)

        zero = jnp.zeros((COUNT_ROWS, LANES), F32)
        cnts = lax.fori_loop(0, nkb, count_chunk, (zero,) * (tq // COUNT_ROWS))
        cnt = jnp.sum(jnp.concatenate(cnts, axis=0), axis=-1, keepdims=True)
        return jnp.where(cnt >= float(topk), cand, thr)

    thr = lax.fori_loop(0, 32, bit_step, jnp.full((tq, 1), INT32_MIN, I32))
    thr = jnp.maximum(thr, NEG_INF_KEY + 1)

    lane = lax.broadcasted_iota(I32, (nh * tq, LANES), 1)
    q_ref[:, HEAD_DIM:] = jnp.where(lane == 0, 1.0, 0.0).astype(BF16)
    for h in range(nh):
        q_ref[h * tq:(h + 1) * tq, :HEAD_DIM] = dq_ref[:, h * HEAD_DIM:(h + 1) * HEAD_DIM]
    acc_ref[...] = jnp.zeros_like(acc_ref)
    if track_max:
        m_ref[...] = jnp.full_like(m_ref, -jnp.inf)

    def attn_chunk(kb, carry):
        ks = pl.multiple_of(kb * tk, tk)
        k = dk_ref[pl.ds(ks, tk), :]
        v = dv_ref[pl.ds(ks, tk), :]
        drop = jnp.where(keys_ref[kb] >= thr, 0.0, -jnp.inf)
        for g in range(groups):
            rows = pl.ds(g * ag * tq, ag * tq)
            s = lax.dot_general(q_ref[rows, :], k, NT_DIMS, preferred_element_type=F32)
            s = (s.reshape(ag, tq, tk) + drop[None]).reshape(ag * tq, tk)
            _attend(s, v, acc_ref.at[rows], m_ref.at[rows], track_max)
        return carry

    lax.fori_loop(0, nkb, attn_chunk, 0)
    out = _normalised(acc_ref[...])
    for h in range(nh):
        o_ref[:, h * HEAD_DIM:(h + 1) * HEAD_DIM] = out[h * tq:(h + 1) * tq, :].astype(o_ref.dtype)


GROUP_ROWS = 1024
COUNT_ROWS = 64


def _dsa_attention(iq, ik, iw, dq, dk, dv, *, topk, track_max, tq=256, tk=512):
    S, W = dq.shape
    nh = W // HEAD_DIM
    tq, tk = min(tq, S), min(tk, S)
    nkb = S // tk
    full = lambda shape: pl.BlockSpec(shape, lambda i: (0,) * len(shape))
    return pl.pallas_call(
        functools.partial(_dsa_body, tq=tq, tk=tk, topk=topk, nh=nh, track_max=track_max),
        grid=(S // tq,),
        in_specs=[
            pl.BlockSpec((IDX_HEADS, tq, IDX_DIM), lambda i: (0, i, 0)),
            full((S, IDX_DIM)),
            pl.BlockSpec((tq, IDX_HEADS), lambda i: (i, 0)),
            pl.BlockSpec((tq, W), lambda i: (i, 0)),
            full((S, WIDE)),
            full((S, WIDE)),
        ],
        out_specs=pl.BlockSpec((tq, W), lambda i: (i, 0)),
        out_shape=jax.ShapeDtypeStruct((S, W), BF16),
        scratch_shapes=[
            pltpu.VMEM((nkb, tq, tk), I32),
            pltpu.VMEM((IDX_HEADS * tq, 1), F32),
            pltpu.VMEM((nh * tq, WIDE), BF16),
            pltpu.VMEM((nh * tq, WIDE), F32),
            pltpu.VMEM((nh * tq, LANES), F32),
        ],
        compiler_params=_params(58, "arbitrary"),
        name="dsa_attention",
    )(iq, ik, iw, dq, dk, dv)


ROUTE_GROUP = 8


NOT_RANKED = float(PEER_NKEYS - 1)


def _top_rows(s, n, with_rank=False):
    rows = []
    rank = jnp.full(s.shape, NOT_RANKED, F32)
    for k in range(n):
        m = jnp.max(s, axis=0, keepdims=True)
        rows.append(m)
        hit = s == m
        if with_rank:
            rank = jnp.where(hit, float(k), rank)
        s = jnp.where(hit, -jnp.inf, s)
    return (rows, rank) if with_rank else rows


def _route_body(qt_ref, kk_ref, n1_ref, e1_ref, r2_ref, e2_ref):
    ts = qt_ref.shape[1]
    ng = PEER_NKEYS // ROUTE_GROUP
    for h in range(PEER_HEADS):
        qh = qt_ref[h * LANES:(h + 1) * LANES, :]
        sc = jnp.dot(kk_ref[h], qh, preferred_element_type=F32, precision=lax.Precision.HIGHEST)
        s1, s2 = sc[:PEER_NKEYS], sc[PEER_NKEYS:]
        v1 = _top_rows(s1, PEER_TOPK)
        v2, rank2 = _top_rows(s2, PEER_TOPK, with_rank=True)
        v2 = jnp.concatenate(v2, axis=0)
        cand = [v1[a] + v2[:-(-(PEER_TOPK // (a + 1)) // 8) * 8] for a in range(PEER_TOPK)]
        best = _top_rows(jnp.concatenate(cand, axis=0), PEER_TOPK)
        tau = best[PEER_TOPK - 1]
        z = sum(jnp.exp(b - best[0]) for b in best)
        n1 = jnp.zeros((PEER_NKEYS, ts), F32)
        for a in range(PEER_TOPK):
            n_a = jnp.sum(jnp.where(cand[a] >= tau, 1.0, 0.0), axis=0, keepdims=True)
            n1 = jnp.where(s1 == v1[a], n_a, n1)
        n1_ref[:, h] = n1.reshape(ng, ROUTE_GROUP, ts)
        e1_ref[:, h] = (jnp.exp(s1 - v1[0]) / z).reshape(ng, ROUTE_GROUP, ts)
        r2_ref[h] = rank2.astype(BF16)
        e2_ref[h] = jnp.exp(s2 - v2[0:1]).astype(BF16)


def _peer_route(qt, kk, ts=256):
    NQ, S = qt.shape
    ts = min(ts, S)
    ng = PEER_NKEYS // ROUTE_GROUP
    grouped = jax.ShapeDtypeStruct((ng, PEER_HEADS, ROUTE_GROUP, S), F32)
    whole = jax.ShapeDtypeStruct((PEER_HEADS, PEER_NKEYS, S), BF16)
    gspec = pl.BlockSpec((ng, PEER_HEADS, ROUTE_GROUP, ts), lambda i: (0, 0, 0, i))
    wspec = pl.BlockSpec((PEER_HEADS, PEER_NKEYS, ts), lambda i: (0, 0, i))
    return pl.pallas_call(
        _route_body,
        grid=(S // ts,),
        in_specs=[pl.BlockSpec((NQ, ts), lambda i: (0, i)),
                  pl.BlockSpec((PEER_HEADS, 2 * PEER_NKEYS, LANES), lambda i: (0, 0, 0))],
        out_specs=(gspec, gspec, wspec, wspec),
        out_shape=(grouped, grouped, whole, whole),
        compiler_params=_params(40, "arbitrary"),
        name="peer_route",
    )(qt, kk)


def _experts_body(x_ref, u_ref, v_ref, n1_ref, e1_ref, r2_ref, e2_ref, res_ref, o_ref, *, nsub):
    eb = pl.program_id(1)

    @pl.when(eb == 0)
    def _():
        o_ref[...] = jnp.zeros_like(o_ref)

    rr = res_ref.shape[0]
    r0 = pl.multiple_of(eb * rr, rr)
    o_ref[pl.ds(r0, rr), :] += res_ref[...]

    first = (eb * nsub) % ROUTE_GROUP
    tb = x_ref.shape[0]
    ht = lax.dot_general(u_ref[...], x_ref[...], NT_DIMS, preferred_element_type=F32)
    parts = []
    reps = PEER_NKEYS // BF16_ROWS

    def row_tile(ref, h, row):
        packed = jnp.broadcast_to(ref[h, pl.ds(row, 1), :], (BF16_ROWS, tb)).astype(BF16)
        return jnp.tile(packed, (reps, 1))

    for j in range(nsub):
        gate = jnp.zeros((PEER_NKEYS, tb), BF16)
        for h in range(PEER_HEADS):
            routed = r2_ref[h] < row_tile(n1_ref, h, first + j)
            gate = gate + jnp.where(routed, e2_ref[h] * row_tile(e1_ref, h, first + j), 0.0)
        hj = ht[j * PEER_NKEYS:(j + 1) * PEER_NKEYS, :]
        act = 0.5 * hj * (1.0 + lax.erf(hj * (2.0 ** -0.5)))
        parts.append(gate.astype(F32) * act)
    a = jnp.concatenate(parts, axis=0).T.astype(BF16)
    o_ref[...] += jnp.dot(a, v_ref[...], preferred_element_type=F32)


BF16_ROWS = 16


def _peer_experts(h2, u, v, n1, e1, r2, e2, res, tb=512, te=1024):
    S, D = h2.shape
    E = u.shape[0]
    tb = min(tb, S)
    nsub = te // PEER_NKEYS
    ne = E // te
    rr = tb // ne
    assert ROUTE_GROUP % nsub == 0 and tb % ne == 0 and rr % 8 == 0
    grouped = pl.BlockSpec((None, PEER_HEADS, ROUTE_GROUP, tb), lambda i, j: ((j * nsub) // ROUTE_GROUP, 0, 0, i))
    once = pl.Buffered(1)
    whole = pl.BlockSpec((PEER_HEADS, PEER_NKEYS, tb), lambda i, j: (0, 0, i), pipeline_mode=once)
    return pl.pallas_call(
        functools.partial(_experts_body, nsub=nsub),
        grid=(S // tb, ne),
        in_specs=[
            pl.BlockSpec((tb, D), lambda i, j: (i, 0), pipeline_mode=once),
            pl.BlockSpec((te, D), lambda i, j: (j, 0)),
            pl.BlockSpec((te, D), lambda i, j: (j, 0)),
            grouped, grouped, whole, whole,
            pl.BlockSpec((rr, D), lambda i, j: (i * ne + j, 0)),
        ],
        out_specs=pl.BlockSpec((tb, D), lambda i, j: (i, 0), pipeline_mode=once),
        out_shape=jax.ShapeDtypeStruct((S, D), F32),
        compiler_params=_params(60, "arbitrary", "arbitrary"),
        name="peer_experts",
    )(h2, u, v, n1, e1, r2, e2, res)


def _relayout_w_tail(w_in, D):
    fh, W, off, total = _tail_layout(D)
    sizes = (fh, W, HEAD_DIM, HEAD_DIM, IDX_HEADS * IDX_DIM, IDX_DIM, IDX_HEADS)
    pts = (3 * W + np.cumsum(sizes)[:-1]).tolist()
    _, ff, dq, dk, dv, iq, ik, iw = jnp.split(w_in, [3 * W] + pts, axis=1)
    zeros = lambda n: jnp.zeros((D, n), w_in.dtype)
    misc = jnp.concatenate([ik, iw, zeros(LANES - IDX_DIM - IDX_HEADS)], axis=1)
    ffp = jnp.concatenate([ff, zeros(LANES - fh)], axis=1)
    cols = [dq, iq, dk, dv, misc, ffp]
    used = sum(c.shape[1] for c in cols)
    if total > used:
        cols.append(zeros(total - used))
    return jnp.concatenate(cols, axis=1).astype(BF16)


def _layer(x, positions, ln1_g, w_in, fox_forget_b, fox_qn_g, fox_kn_g, dsa_qn_g, dsa_kn_g,
           w_o, ln2_g, peer_wq, peer_keys1, peer_keys2, peer_u, peer_v):
    S, D = x.shape
    W = D // 2
    topk = min(DSA_TOPK_MAX, S // 4)

    h1 = _rmsnorm(x, ln1_g, BF16)
    proj_a = _matmul(h1, w_in[:, :3 * W].astype(BF16), name="in_proj_fox")
    proj_b = _matmul(h1, _relayout_w_tail(w_in, D), name="in_proj_dsa")
    fox_safe, fox_half = _logit_span(fox_qn_g, fox_kn_g)
    dsa_safe, dsa_half = _logit_span(dsa_qn_g, dsa_kn_g)
    fq, fk, fv, dq, dk, dv, iq, ik, iw = _post(proj_a, proj_b, positions, fox_forget_b, fox_qn_g, fox_kn_g,
                                               dsa_qn_g, dsa_kn_g, fox_half, dsa_half, D)
    fox_out = lax.cond(fox_safe, functools.partial(_fox_attention, track_max=False),
                       functools.partial(_fox_attention, track_max=True), fq, fk, fv)
    dsa_out = lax.cond(dsa_safe, functools.partial(_dsa_attention, topk=topk, track_max=False),
                       functools.partial(_dsa_attention, topk=topk, track_max=True), iq, ik, iw, dq, dk, dv)
    x1 = _out_proj(fox_out, dsa_out, w_o, x)

    h2 = _rmsnorm(x1, ln2_g, BF16)
    qt = _matmul(peer_wq.T.astype(BF16), h2, nt=True, name="peer_query")
    zk = jnp.zeros_like(peer_keys1)
    kk = jnp.concatenate([jnp.concatenate([peer_keys1, zk], axis=2),
                          jnp.concatenate([zk, peer_keys2], axis=2)], axis=1).astype(F32)
    n1, e1, r2, e2 = _peer_route(qt, kk)
    return _peer_experts(h2, peer_u.astype(BF16), peer_v.astype(BF16), n1, e1, r2, e2, x1)


def kernel(x, positions, ln1_g, w_in, fox_forget_b, fox_qn_g, fox_kn_g, dsa_qn_g, dsa_kn_g, w_o, ln2_g, peer_wq,
           peer_keys1, peer_keys2, peer_u, peer_v):
    B = x.shape[0]
    depth = w_in.shape[0]
    outs = []
    for b in range(B):
        xb = x[b]
        for l in range(depth):
            xb = _layer(xb, positions[b], ln1_g[l], w_in[l], fox_forget_b[l], fox_qn_g[l], fox_kn_g[l],
                        dsa_qn_g[l], dsa_kn_g[l], w_o[l], ln2_g[l], peer_wq[l], peer_keys1[l], peer_keys2[l],
                        peer_u[l], peer_v[l])
        outs.append(xb)
    return outs[0][None] if B == 1 else jnp.stack(outs, axis=0)
```
